```python
import math
import jax, jax.numpy as jnp
from jax import lax
import numpy as np

D_MODEL = 1024
BATCH = 8
SEQ = 4096
DEPTH = 1

N_META = 16
CONV_WIDTH = 1024
CONV_GROUPS = 16
CONV_K = 3
N_DIFF_HEADS = 8
DIFF_HEAD_DIM = 64
ATTN_QK_WIDTH = N_DIFF_HEADS * 2 * DIFF_HEAD_DIM
ATTN_V_WIDTH = N_DIFF_HEADS * 2 * DIFF_HEAD_DIM
IN_PROJ_WIDTH = 3 * CONV_WIDTH + 2 * ATTN_QK_WIDTH + ATTN_V_WIDTH + 2 * D_MODEL
D_FF = 2816
FFN_CONV_K = 3
Q_BLOCK = 128
RMS_EPS = 1e-6
NEG_INF = -1e30

kernel_name = "hybrid_shortconv_diffattn_gated_merge"


def rmsnorm(x, g):
    xf = x.astype(jnp.float32)
    y = xf * lax.rsqrt(jnp.mean(xf * xf, axis=-1, keepdims=True) + RMS_EPS)
    return (y * g.astype(jnp.float32)).astype(x.dtype)


def causal_dwconv(x, w, b=None):
    K = w.shape[0]
    L = x.shape[1]
    xp = jnp.pad(x, ((0, 0), (K - 1, 0), (0, 0)))
    y = xp[:, 0:L] * w[0]
    for k in range(1, K):
        y = y + xp[:, k:k + L] * w[k]
    if b is not None:
        y = y + b
    return y


def alibi_slopes(n_heads):
    return 2.0 ** (-8.0 * jnp.arange(1, n_heads + 1, dtype=jnp.float32) / n_heads)


def diff_attention(q, k, v, lam, lam_init, subln_g):
    B, L, H, _, dh = q.shape
    Lp = -(-L // Q_BLOCK) * Q_BLOCK
    q = jnp.pad(q, ((0, 0), (0, Lp - L), (0, 0), (0, 0), (0, 0)))
    k = jnp.pad(k, ((0, 0), (0, Lp - L), (0, 0), (0, 0), (0, 0)))
    v = jnp.pad(v, ((0, 0), (0, Lp - L), (0, 0), (0, 0)))
    nb = Lp // Q_BLOCK
    qb = q.reshape(B, nb, Q_BLOCK, H, 2, dh).transpose(1, 0, 2, 3, 4, 5)
    slopes = alibi_slopes(H)
    k_pos = jnp.arange(Lp)
    scale = dh ** -0.5

    def block(args):
        qi, i = args
        q_pos = i * Q_BLOCK + jnp.arange(Q_BLOCK)
        dist = (q_pos[:, None] - k_pos[None, :]).astype(jnp.float32)
        bias = -slopes[:, None, None] * dist
        causal = k_pos[None, :] <= q_pos[:, None]
        s = jnp.einsum('bqhmd,bkhmd->bhmqk', qi, k).astype(jnp.float32) * scale
        s = jnp.where(causal, s + bias[None, :, None], NEG_INF)
        p = jax.nn.softmax(s, axis=-1)
        w = p[:, :, 0] - lam * p[:, :, 1]
        return jnp.einsum('bhqk,bkhe->bqhe', w.astype(v.dtype), v)

    o = lax.map(block, (qb, jnp.arange(nb)))
    o = o.transpose(1, 0, 2, 3, 4).reshape(B, Lp, H, 2 * dh)[:, :L]
    o = rmsnorm(o, subln_g) * (1.0 - lam_init)
    return o.reshape(B, L, H * 2 * dh)


def setup_inputs(seed: int = 0) -> dict:
    key = jax.random.key(seed)
    ks = jax.random.split(key, 24)
    f32 = jnp.float32

    def nrm(k, shape, scale):
        return jax.random.normal(k, shape, f32) * scale

    def gain(k, shape):
        return 1.0 + 0.05 * jax.random.normal(k, shape, f32)

    L = DEPTH
    return {
        "x": nrm(ks[0], (BATCH, SEQ, D_MODEL), 1.0),
        "meta_tokens": nrm(ks[1], (N_META, D_MODEL), 1.0),
        "w_in": nrm(ks[2], (L, D_MODEL, IN_PROJ_WIDTH), D_MODEL ** -0.5),
        "conv_w": nrm(ks[3], (L, CONV_K, CONV_WIDTH), CONV_K ** -0.5),
        "w_conv_out": nrm(ks[4], (L, CONV_WIDTH, D_MODEL), CONV_WIDTH ** -0.5),
        "lambda_q1": nrm(ks[5], (L, DIFF_HEAD_DIM), 0.1),
        "lambda_k1": nrm(ks[6], (L, DIFF_HEAD_DIM), 0.1),
        "lambda_q2": nrm(ks[7], (L, DIFF_HEAD_DIM), 0.1),
        "lambda_k2": nrm(ks[8], (L, DIFF_HEAD_DIM), 0.1),
        "subln_g": gain(ks[9], (L, 2 * DIFF_HEAD_DIM)),
        "w_attn_out": nrm(ks[10], (L, ATTN_V_WIDTH, D_MODEL), ATTN_V_WIDTH ** -0.5),
        "w_mix_out": nrm(ks[11], (L, D_MODEL, D_MODEL), D_MODEL ** -0.5),
        "norm_mix_pre": gain(ks[12], (L, D_MODEL)),
        "norm_mix_post": gain(ks[13], (L, D_MODEL)),
        "w_ffn_up": nrm(ks[14], (L, D_MODEL, 2 * D_FF), D_MODEL ** -0.5),
        "ffn_conv_w": nrm(ks[15], (L, FFN_CONV_K, 2 * D_FF), FFN_CONV_K ** -0.5),
        "ffn_conv_b": nrm(ks[16], (L, 2 * D_FF), 0.01),
        "w_ffn_down": nrm(ks[17], (L, D_FF, D_MODEL), D_FF ** -0.5),
        "norm_ffn_pre": gain(ks[18], (L, D_MODEL)),
        "norm_ffn_post": gain(ks[19], (L, D_MODEL)),
    }


def reference(x, meta_tokens, w_in, conv_w, w_conv_out, lambda_q1, lambda_k1, lambda_q2, lambda_k2,
              subln_g, w_attn_out, w_mix_out, norm_mix_pre, norm_mix_post, w_ffn_up, ffn_conv_w,
              ffn_conv_b, w_ffn_down, norm_ffn_pre, norm_ffn_post):
    B = x.shape[0]
    h = jnp.concatenate(
        [jnp.broadcast_to(meta_tokens[None].astype(x.dtype), (B, N_META, D_MODEL)), x], axis=1)
    Ltot = h.shape[1]
    sizes = [CONV_WIDTH] * 3 + [ATTN_QK_WIDTH] * 2 + [ATTN_V_WIDTH] + [D_MODEL] * 2
    split_idx = np.cumsum(sizes)[:-1].tolist()
    for l in range(DEPTH):
        lam_init = 0.8 - 0.6 * math.exp(-0.3 * l)
        a = rmsnorm(h, norm_mix_pre[l])
        proj = a @ w_in[l]
        cb, cc, cx, q, k, v, ga, gb = jnp.split(proj, split_idx, axis=-1)
        ya = (cb * causal_dwconv(cc * cx, conv_w[l])) @ w_conv_out[l]
        lam = (jnp.exp(jnp.sum(lambda_q1[l].astype(jnp.float32) * lambda_k1[l].astype(jnp.float32)))
               - jnp.exp(jnp.sum(lambda_q2[l].astype(jnp.float32) * lambda_k2[l].astype(jnp.float32)))
               + lam_init)
        q = q.reshape(B, Ltot, N_DIFF_HEADS, 2, DIFF_HEAD_DIM)
        k = k.reshape(B, Ltot, N_DIFF_HEADS, 2, DIFF_HEAD_DIM)
        v = v.reshape(B, Ltot, N_DIFF_HEADS, 2 * DIFF_HEAD_DIM)
        yb = diff_attention(q, k, v, lam, lam_init, subln_g[l]) @ w_attn_out[l]
        mix = (jax.nn.sigmoid(ga) * ya + jax.nn.sigmoid(gb) * yb) @ w_mix_out[l]
        h = h + rmsnorm(mix, norm_mix_post[l])
        f = rmsnorm(h, norm_ffn_pre[l])
        u = causal_dwconv(f @ w_ffn_up[l], ffn_conv_w[l], ffn_conv_b[l])
        g, u = jnp.split(u, 2, axis=-1)
        y = (jax.nn.silu(g) * u) @ w_ffn_down[l]
        h = h + rmsnorm(y, norm_ffn_post[l])
    return h[:, N_META:]
```

```python
import functools
import math

import numpy as np
import jax
import jax.numpy as jnp
from jax import lax
from jax.experimental import pallas as pl
from jax.experimental.pallas import tpu as pltpu

F32 = jnp.float32
BF16 = jnp.bfloat16

N_META = 16
N_HEADS = 8
HEAD_DIM = 64
V_DIM = 2 * HEAD_DIM
N_SEG = 8
RMS_EPS = 1e-6
NEG = -1e30

LANES = 128
SUBLANES = 8
ROW_TILE = 512
COL_CHUNK = 256
Q_TILE = 256
PREFIX = 128
ACC_ROWS = V_DIM + 16
VMEM_LIMIT = 56 * 1024 * 1024

LANE_POS_HI = HEAD_DIM
LANE_POS_LO = HEAD_DIM + 1
LANE_VALID = HEAD_DIM + 2
POS_SPLIT = 64


def _rms(x, g):
    ms = jnp.mean(x * x, axis=-1, keepdims=True)
    return x * lax.rsqrt(ms + RMS_EPS) * g


def _dot(a, b):
    return jnp.dot(a, b, preferred_element_type=F32)


def _causal_conv3(z, w, buf_ref, carry_ref, cols):
    tm = z.shape[0]
    buf_ref[0:SUBLANES, :] = carry_ref[:, cols]
    buf_ref[SUBLANES:SUBLANES + tm, :] = z
    y = (w[2:3] * z + w[1:2] * buf_ref[SUBLANES - 1:SUBLANES - 1 + tm, :]
         + w[0:1] * buf_ref[SUBLANES - 2:SUBLANES - 2 + tm, :])
    carry_ref[:, cols] = buf_ref[tm:tm + SUBLANES, :]
    return y


def _in_proj_body(x_ref, g_ref, w_ref, cw_ref, hist_ref,
                  ypre_ref, q_ref, k_ref, v_ref, sga_ref, sgb_ref, tail_ref,
                  xn_ref, buf_ref, carry_ref, *, tiles_per_seq, width, q_scale):
    @pl.when(pl.program_id(0) % tiles_per_seq == 0)
    def _():
        carry_ref[...] = hist_ref[...]

    xn_ref[...] = _rms(x_ref[...], g_ref[...]).astype(BF16)

    for c in range(width // COL_CHUNK):
        cols = slice(c * COL_CHUNK, (c + 1) * COL_CHUNK)

        def proj(seg):
            lo = seg * width + c * COL_CHUNK
            return _dot(xn_ref[...], w_ref[:, lo:lo + COL_CHUNK])

        u = proj(1) * proj(2)
        y = _causal_conv3(u, cw_ref[:, cols], buf_ref, carry_ref, cols)
        ypre_ref[:, cols] = (proj(0) * y).astype(BF16)
        q_ref[:, cols] = (proj(3) * q_scale).astype(BF16)
        k_ref[:, cols] = proj(4).astype(BF16)
        v_ref[:, cols] = proj(5).astype(BF16)
        sga_ref[:, cols] = jax.nn.sigmoid(proj(6)).astype(BF16)
        sgb_ref[:, cols] = jax.nn.sigmoid(proj(7)).astype(BF16)

    tail_ref[...] = carry_ref[...]


def _resident(shape):
    return pl.BlockSpec(shape, lambda i: (0,) * len(shape), pipeline_mode=pl.Buffered(1))


def _in_proj(h, gain, w_in, conv_w, hist, *, seq_rows):
    rows, d = h.shape
    width = w_in.shape[1] // N_SEG
    tm = min(ROW_TILE, rows)
    assert rows % tm == 0 and seq_rows % tm == 0 and width % COL_CHUNK == 0
    row_spec = lambda n: pl.BlockSpec((tm, n), lambda i: (i, 0))
    act = jax.ShapeDtypeStruct((rows, width), BF16)
    return pl.pallas_call(
        functools.partial(_in_proj_body, tiles_per_seq=seq_rows // tm, width=width,
                          q_scale=HEAD_DIM ** -0.5),
        grid=(rows // tm,),
        in_specs=[row_spec(d), _resident((1, d)), _resident(w_in.shape), _resident(conv_w.shape),
                  _resident(hist.shape)],
        out_specs=[row_spec(width)] * 6 + [pl.BlockSpec((SUBLANES, width), lambda i: (0, 0))],
        out_shape=[act] * 6 + [jax.ShapeDtypeStruct((SUBLANES, width), F32)],
        scratch_shapes=[pltpu.VMEM((tm, d), BF16),
                        pltpu.VMEM((tm + SUBLANES, COL_CHUNK), F32),
                        pltpu.VMEM((SUBLANES, width), F32)],
        compiler_params=pltpu.CompilerParams(dimension_semantics=("arbitrary",),
                                             vmem_limit_bytes=VMEM_LIMIT),
        name="in_proj",
    )(h, gain, w_in, conv_w, hist)


def _lam(lam_ref, lam_init):
    l = lam_ref[...]
    return (jnp.exp(jnp.sum(l[0:1] * l[1:2], axis=-1, keepdims=True))
            - jnp.exp(jnp.sum(l[2:3] * l[3:4], axis=-1, keepdims=True)) + lam_init)


def _subln(o, g, lam_init):
    return _rms(o, g) * (1.0 - lam_init)


def _key_features(row0, n):
    row = lax.broadcasted_iota(jnp.int32, (n, LANES), 0) + row0
    lane = lax.broadcasted_iota(jnp.int32, (n, LANES), 1)
    pos = row - (PREFIX - N_META)
    hi = (pos >> 6).astype(F32)
    lo = (pos & (POS_SPLIT - 1)).astype(F32)
    valid = jnp.where(pos < 0, NEG, 0.0)
    feat = jnp.where(lane == LANE_POS_HI, hi,
                     jnp.where(lane == LANE_POS_LO, lo,
                               jnp.where(lane == LANE_VALID, valid, 0.0)))
    return feat, lane


def _split_maps(x, feat, lane):
    first = jnp.where(lane < HEAD_DIM, x, feat)
    second = jnp.where(lane < HEAD_DIM, pltpu.roll(x, HEAD_DIM, axis=1), feat)
    return first, second


def _attn_body(q_ref, k_ref, v_ref, mk_ref, mv_ref, slope_ref, lam_ref, g_ref, o_ref,
               kp1_ref, kp2_ref, vtp_ref, ka1_ref, ka2_ref, vt_ref, mask_ref, acc1_ref, acc2_ref,
               *, lam_init):
    qi = pl.program_id(2)
    n_tiles, tk, _ = ka1_ref.shape
    tq = q_ref.shape[0]

    ones_rows = jnp.where(lax.broadcasted_iota(jnp.int32, (ACC_ROWS - V_DIM, LANES), 0) == 0,
                          1.0, 0.0).astype(BF16)

    @pl.when(qi == 0)
    def _build_keys():
        pad = jnp.zeros((PREFIX - N_META, LANES), F32)
        feat, lane = _key_features(0, PREFIX)
        k1, k2 = _split_maps(jnp.concatenate([pad, mk_ref[...].astype(F32)], axis=0), feat, lane)
        kp1_ref[...] = k1.astype(BF16)
        kp2_ref[...] = k2.astype(BF16)
        vtp_ref[0:V_DIM, :] = jnp.concatenate([pad, mv_ref[...].astype(F32)], axis=0).T.astype(BF16)
        vtp_ref[V_DIM:ACC_ROWS, :] = ones_rows

        def tile(t, carry):
            r0 = pl.multiple_of(t * tk, tk)
            feat, lane = _key_features(PREFIX + r0, tk)
            k1, k2 = _split_maps(k_ref[pl.ds(r0, tk), :].astype(F32), feat, lane)
            ka1_ref[t] = k1.astype(BF16)
            ka2_ref[t] = k2.astype(BF16)
            vt_ref[t, 0:V_DIM, :] = v_ref[pl.ds(r0, tk), :].astype(F32).T.astype(BF16)
            vt_ref[t, V_DIM:ACC_ROWS, :] = jnp.concatenate([ones_rows] * (tk // LANES), axis=1)
            return carry

        lax.fori_loop(0, n_tiles, tile, 0)
        kk = lax.broadcasted_iota(jnp.int32, (tk, tq), 0)
        qq = lax.broadcasted_iota(jnp.int32, (tk, tq), 1)
        mask_ref[...] = jnp.where(kk <= qq, 0.0, NEG)

    lane = lax.broadcasted_iota(jnp.int32, (tq, LANES), 1)
    slope = slope_ref[...]
    qfeat = jnp.where(lane == LANE_POS_HI, slope * POS_SPLIT,
                      jnp.where(lane == LANE_POS_LO, slope,
                                jnp.where(lane == LANE_VALID, 1.0, 0.0)))
    q1, q2 = _split_maps(q_ref[...].astype(F32), qfeat, lane)
    qt1 = q1.T.astype(BF16)
    qt2 = q2.T.astype(BF16)

    acc1_ref[...] = jnp.zeros_like(acc1_ref)
    acc2_ref[...] = jnp.zeros_like(acc2_ref)

    def one_map(keys, vt, qt, acc_ref, m, mask):
        s = _dot(keys, qt)
        if mask is not None:
            s = s + mask
        m_new = jnp.maximum(m, jnp.max(s, axis=0, keepdims=True))
        p = jnp.exp(s - m_new).astype(BF16)
        acc_ref[...] = jnp.exp(m - m_new) * acc_ref[...] + _dot(vt, p)
        return m_new

    def step(k1, k2, vt, ms, mask=None):
        return (one_map(k1, vt, qt1, acc1_ref, ms[0], mask),
                one_map(k2, vt, qt2, acc2_ref, ms[1], mask))

    m0 = jnp.full((1, tq), NEG, F32)
    ms = step(kp1_ref[...], kp2_ref[...], vtp_ref[...], (m0, m0))
    ms = lax.fori_loop(0, qi, lambda t, ms: step(ka1_ref[t], ka2_ref[t], vt_ref[t], ms), ms)
    step(ka1_ref[qi], ka2_ref[qi], vt_ref[qi], ms, mask_ref[...])

    a1 = acc1_ref[...]
    a2 = acc2_ref[...]
    o_t = (a1[0:V_DIM] / a1[V_DIM:V_DIM + 1]
           - _lam(lam_ref, lam_init) * (a2[0:V_DIM] / a2[V_DIM:V_DIM + 1]))
    o_ref[...] = _subln(o_t.T, g_ref[...], lam_init).astype(BF16)


def _attention(q, k, v, mk, mv, slopes, lam_vecs, g, *, batch, seq, lam_init):
    rows, width = q.shape
    tq = Q_TILE
    n_tiles = seq // tq
    assert seq % tq == 0 and width == N_HEADS * V_DIM and tq % LANES == 0
    full = lambda a: pl.BlockSpec(a.shape, lambda b, h, i: (0,) * a.ndim)
    q_spec = pl.BlockSpec((tq, V_DIM), lambda b, h, i: (b * n_tiles + i, h))
    kv_spec = pl.BlockSpec((seq, V_DIM), lambda b, h, i: (b, h))
    meta_spec = pl.BlockSpec((N_META, V_DIM), lambda b, h, i: (0, h))
    return pl.pallas_call(
        functools.partial(_attn_body, lam_init=lam_init),
        grid=(batch, N_HEADS, n_tiles),
        in_specs=[q_spec, kv_spec, kv_spec, meta_spec, meta_spec,
                  pl.BlockSpec((None, 1, LANES), lambda b, h, i: (h, 0, 0)), full(lam_vecs), full(g)],
        out_specs=q_spec,
        out_shape=jax.ShapeDtypeStruct((rows, width), BF16),
        scratch_shapes=[pltpu.VMEM((PREFIX, LANES), BF16), pltpu.VMEM((PREFIX, LANES), BF16),
                        pltpu.VMEM((ACC_ROWS, PREFIX), BF16),
                        pltpu.VMEM((n_tiles, tq, LANES), BF16), pltpu.VMEM((n_tiles, tq, LANES), BF16),
                        pltpu.VMEM((n_tiles, ACC_ROWS, tq), BF16),
                        pltpu.VMEM((tq, tq), F32),
                        pltpu.VMEM((ACC_ROWS, tq), F32), pltpu.VMEM((ACC_ROWS, tq), F32)],
        compiler_params=pltpu.CompilerParams(
            dimension_semantics=("arbitrary", "arbitrary", "arbitrary"),
            vmem_limit_bytes=VMEM_LIMIT),
        name="diff_attention",
    )(q, k, v, mk, mv, slopes, lam_vecs, g)


def _meta_attn_body(q_ref, k_ref, v_ref, slope_ref, lam_ref, g_ref, o_ref, *, lam_init):
    q = q_ref[...]
    k = k_ref[...]
    n = q.shape[0]
    lane = lax.broadcasted_iota(jnp.int32, q.shape, 1)
    zero = jnp.zeros_like(q)
    contract_last = (((1,), (1,)), ((), ()))
    s1 = lax.dot_general(jnp.where(lane < HEAD_DIM, q, zero), k, contract_last,
                         preferred_element_type=F32)
    s2 = lax.dot_general(jnp.where(lane >= HEAD_DIM, q, zero), k, contract_last,
                         preferred_element_type=F32)
    qpos = lax.broadcasted_iota(jnp.int32, (n, n), 0)
    kpos = lax.broadcasted_iota(jnp.int32, (n, n), 1)
    bias = -slope_ref[...][:, 0:1] * (qpos - kpos).astype(F32)

    def softmax(s):
        s = jnp.where(kpos <= qpos, s + bias, NEG)
        e = jnp.exp(s - jnp.max(s, axis=-1, keepdims=True))
        return e / jnp.sum(e, axis=-1, keepdims=True)

    w = softmax(s1) - _lam(lam_ref, lam_init) * softmax(s2)
    o_ref[...] = _subln(_dot(w.astype(BF16), v_ref[...]), g_ref[...], lam_init).astype(BF16)


def _meta_attention(q, k, v, slopes, lam_vecs, g, *, lam_init):
    full = lambda a: pl.BlockSpec(a.shape, lambda h: (0,) * a.ndim)
    head = pl.BlockSpec((N_META, V_DIM), lambda h: (0, h))
    return pl.pallas_call(
        functools.partial(_meta_attn_body, lam_init=lam_init),
        grid=(N_HEADS,),
        in_specs=[head, head, head, pl.BlockSpec((None, 1, LANES), lambda h: (h, 0, 0)),
                  full(lam_vecs), full(g)],
        out_specs=head,
        out_shape=jax.ShapeDtypeStruct(q.shape, BF16),
        name="meta_attention",
    )(q, k, v, slopes, lam_vecs, g)


def _merge_body(ypre_ref, o_ref, sga_ref, sgb_ref, h_ref, wc_ref, wa_ref, wm_ref, g_ref, out_ref):
    ya = _dot(ypre_ref[...], wc_ref[...])
    yb = _dot(o_ref[...], wa_ref[...])
    gated = sga_ref[...].astype(F32) * ya + sgb_ref[...].astype(F32) * yb
    mix = _dot(gated.astype(BF16), wm_ref[...])
    out_ref[...] = h_ref[...] + _rms(mix, g_ref[...])


def _merge(ypre, o, sga, sgb, h, wc, wa, wm, g):
    rows, d = h.shape
    tm = min(ROW_TILE, rows)
    assert rows % tm == 0
    row_spec = lambda n: pl.BlockSpec((tm, n), lambda i: (i, 0))
    return pl.pallas_call(
        _merge_body,
        grid=(rows // tm,),
        in_specs=[row_spec(ypre.shape[1]), row_spec(o.shape[1]), row_spec(sga.shape[1]),
                  row_spec(sgb.shape[1]), row_spec(d),
                  _resident(wc.shape), _resident(wa.shape), _resident(wm.shape), _resident(g.shape)],
        out_specs=row_spec(d),
        out_shape=jax.ShapeDtypeStruct((rows, d), F32),
        compiler_params=pltpu.CompilerParams(dimension_semantics=("arbitrary",),
                                             vmem_limit_bytes=VMEM_LIMIT),
        name="merge",
    )(ypre, o, sga, sgb, h, wc, wa, wm, g)


def _ffn_body(h_ref, gpre_ref, wup_ref, cw_ref, cb_ref, hist_ref, wdn_ref, gpost_ref,
              out_ref, tail_ref, f_ref, buf_ref, carry_ref, acc_ref, *, tiles_per_seq, d_ff):
    @pl.when(pl.program_id(0) % tiles_per_seq == 0)
    def _():
        carry_ref[...] = hist_ref[...]

    f_ref[...] = _rms(h_ref[...], gpre_ref[...]).astype(BF16)

    for c in range(d_ff // COL_CHUNK):
        halves = []
        for half in range(2):
            lo = half * d_ff + c * COL_CHUNK
            cols = slice(lo, lo + COL_CHUNK)
            z = _dot(f_ref[...], wup_ref[:, cols])
            halves.append(_causal_conv3(z, cw_ref[:, cols], buf_ref, carry_ref, cols) + cb_ref[:, cols])
        a = (jax.nn.silu(halves[0]) * halves[1]).astype(BF16)
        y = _dot(a, wdn_ref[c * COL_CHUNK:(c + 1) * COL_CHUNK, :])
        if c == 0:
            acc_ref[...] = y
        else:
            acc_ref[...] += y

    tail_ref[...] = carry_ref[...]
    out_ref[...] = h_ref[...] + _rms(acc_ref[...], gpost_ref[...])


def _ffn(h, gpre, wup, conv_w, conv_b, hist, wdn, gpost, *, seq_rows):
    rows, d = h.shape
    d_ff = wdn.shape[0]
    tm = min(ROW_TILE, rows)
    assert rows % tm == 0 and seq_rows % tm == 0 and d_ff % COL_CHUNK == 0
    row_spec = pl.BlockSpec((tm, d), lambda i: (i, 0))
    return pl.pallas_call(
        functools.partial(_ffn_body, tiles_per_seq=seq_rows // tm, d_ff=d_ff),
        grid=(rows // tm,),
        in_specs=[row_spec, _resident(gpre.shape), _resident(wup.shape), _resident(conv_w.shape),
                  _resident(conv_b.shape), _resident(hist.shape), _resident(wdn.shape),
                  _resident(gpost.shape)],
        out_specs=[row_spec, pl.BlockSpec((SUBLANES, 2 * d_ff), lambda i: (0, 0))],
        out_shape=[jax.ShapeDtypeStruct((rows, d), F32),
                   jax.ShapeDtypeStruct((SUBLANES, 2 * d_ff), F32)],
        scratch_shapes=[pltpu.VMEM((tm, d), BF16),
                        pltpu.VMEM((tm + SUBLANES, COL_CHUNK), F32),
                        pltpu.VMEM((SUBLANES, 2 * d_ff), F32),
                        pltpu.VMEM((tm, d), F32)],
        compiler_params=pltpu.CompilerParams(dimension_semantics=("arbitrary",),
                                             vmem_limit_bytes=VMEM_LIMIT),
        name="ffn",
    )(h, gpre, wup, conv_w, conv_b, hist, wdn, gpost)


def _alibi_slopes():
    slopes = 2.0 ** (-8.0 * np.arange(1, N_HEADS + 1, dtype=np.float64) / N_HEADS)
    assert np.array_equal(slopes.astype(BF16).astype(np.float64), slopes), "slopes must be exact in bf16"
    return jnp.asarray(np.broadcast_to(slopes[:, None, None], (N_HEADS, 1, LANES)), F32)


def kernel(x, meta_tokens, w_in, conv_w, w_conv_out, lambda_q1, lambda_k1, lambda_q2, lambda_k2,
           subln_g, w_attn_out, w_mix_out, norm_mix_pre, norm_mix_post, w_ffn_up, ffn_conv_w,
           ffn_conv_b, w_ffn_down, norm_ffn_pre, norm_ffn_post):
    batch, seq, d = x.shape
    depth = w_in.shape[0]
    assert meta_tokens.shape[0] == N_META and seq + N_META <= POS_SPLIT * 256
    slopes = _alibi_slopes()
    row = lambda a: a.reshape(1, -1)

    hx = x.reshape(batch * seq, d)
    hm = meta_tokens.astype(x.dtype)
    for l in range(depth):
        lam_init = 0.8 - 0.6 * math.exp(-0.3 * l)
        w_in_l = w_in[l].astype(BF16)
        lam_vecs = jnp.stack([lambda_q1[l], lambda_k1[l], lambda_q2[l], lambda_k2[l]]).astype(F32)
        g_sub = row(subln_g[l])
        zero_hist = jnp.zeros((SUBLANES, conv_w.shape[2]), F32)

        ypre_m, q_m, k_m, v_m, sga_m, sgb_m, u_tail = _in_proj(
            hm, row(norm_mix_pre[l]), w_in_l, conv_w[l], zero_hist, seq_rows=N_META)
        ypre, q, k, v, sga, sgb, _ = _in_proj(
            hx, row(norm_mix_pre[l]), w_in_l, conv_w[l], u_tail, seq_rows=seq)

        o_m = _meta_attention(q_m, k_m, v_m, slopes, lam_vecs, g_sub, lam_init=lam_init)
        o = _attention(q, k, v, k_m, v_m, slopes, lam_vecs, g_sub,
                       batch=batch, seq=seq, lam_init=lam_init)

        wc, wa, wm = (w.astype(BF16) for w in (w_conv_out[l], w_attn_out[l], w_mix_out[l]))
        hm = _merge(ypre_m, o_m, sga_m, sgb_m, hm, wc, wa, wm, row(norm_mix_post[l]))
        hx = _merge(ypre, o, sga, sgb, hx, wc, wa, wm, row(norm_mix_post[l]))

        wup, wdn = w_ffn_up[l].astype(BF16), w_ffn_down[l].astype(BF16)
        zero_hist = jnp.zeros((SUBLANES, wup.shape[1]), F32)
        ffn = functools.partial(_ffn, gpre=row(norm_ffn_pre[l]), wup=wup, conv_w=ffn_conv_w[l],
                                conv_b=row(ffn_conv_b[l]), wdn=wdn, gpost=row(norm_ffn_post[l]))
        hm, z_tail = ffn(hm, hist=zero_hist, seq_rows=N_META)
        hx, _ = ffn(hx, hist=z_tail, seq_rows=seq)
    return hx.reshape(batch, seq, d)
```

```python
import functools
import math

import numpy as np
import jax
import jax.numpy as jnp
from jax import lax
from jax.experimental import pallas as pl
from jax.experimental.pallas import tpu as pltpu

F32 = jnp.float32
BF16 = jnp.bfloat16

N_META = 16
N_HEADS = 8
HEAD_DIM = 64
V_DIM = 2 * HEAD_DIM
N_SEG = 8
RMS_EPS = 1e-6
NEG = -1e30

LANES = 128
SUBLANES = 8
ROW_TILE = 512
COL_CHUNK = 256
Q_TILE = 512
K_TILE = 256
ACC_ROWS = V_DIM + 16
VMEM_LIMIT = 56 * 1024 * 1024

LANE_POS_HI = HEAD_DIM
LANE_POS_LO = HEAD_DIM + 1
POS_SPLIT = 64


def _rms(x, g):
    ms = jnp.mean(x * x, axis=-1, keepdims=True)
    return x * lax.rsqrt(ms + RMS_EPS) * g


def _dot(a, b):
    return jnp.dot(a, b, preferred_element_type=F32)


def _causal_conv3(z, w, buf_ref, carry_ref, cols):
    tm = z.shape[0]
    buf_ref[0:SUBLANES, :] = carry_ref[:, cols]
    buf_ref[SUBLANES:SUBLANES + tm, :] = z
    y = (w[2:3] * z + w[1:2] * buf_ref[SUBLANES - 1:SUBLANES - 1 + tm, :]
         + w[0:1] * buf_ref[SUBLANES - 2:SUBLANES - 2 + tm, :])
    carry_ref[:, cols] = buf_ref[tm:tm + SUBLANES, :]
    return y


def _in_proj_body(x_ref, g_ref, w_ref, cw_ref, hist_ref,
                  ypre_ref, q_ref, k_ref, v_ref, sga_ref, sgb_ref, tail_ref,
                  xn_ref, buf_ref, carry_ref, *, tiles_per_seq, width, q_scale):
    @pl.when(pl.program_id(0) % tiles_per_seq == 0)
    def _():
        carry_ref[...] = hist_ref[...]

    xn_ref[...] = _rms(x_ref[...], g_ref[...]).astype(BF16)

    for c in range(width // COL_CHUNK):
        cols = slice(c * COL_CHUNK, (c + 1) * COL_CHUNK)

        def proj(seg):
            lo = seg * width + c * COL_CHUNK
            return _dot(xn_ref[...], w_ref[:, lo:lo + COL_CHUNK])

        u = proj(1) * proj(2)
        y = _causal_conv3(u, cw_ref[:, cols], buf_ref, carry_ref, cols)
        ypre_ref[:, cols] = (proj(0) * y).astype(BF16)
        q_ref[:, cols] = (proj(3) * q_scale).astype(BF16)
        k_ref[:, cols] = proj(4).astype(BF16)
        v_ref[:, cols] = proj(5).astype(BF16)
        sga_ref[:, cols] = jax.nn.sigmoid(proj(6)).astype(BF16)
        sgb_ref[:, cols] = jax.nn.sigmoid(proj(7)).astype(BF16)

    tail_ref[...] = carry_ref[...]


def _resident(shape):
    return pl.BlockSpec(shape, lambda i: (0,) * len(shape), pipeline_mode=pl.Buffered(1))


def _in_proj(h, gain, w_in, conv_w, hist, *, seq_rows):
    rows, d = h.shape
    width = w_in.shape[1] // N_SEG
    tm = min(ROW_TILE, rows)
    assert rows % tm == 0 and seq_rows % tm == 0 and width % COL_CHUNK == 0
    row_spec = lambda n: pl.BlockSpec((tm, n), lambda i: (i, 0))
    act = jax.ShapeDtypeStruct((rows, width), BF16)
    return pl.pallas_call(
        functools.partial(_in_proj_body, tiles_per_seq=seq_rows // tm, width=width,
                          q_scale=HEAD_DIM ** -0.5),
        grid=(rows // tm,),
        in_specs=[row_spec(d), _resident((1, d)), _resident(w_in.shape), _resident(conv_w.shape),
                  _resident(hist.shape)],
        out_specs=[row_spec(width)] * 6 + [pl.BlockSpec((SUBLANES, width), lambda i: (0, 0))],
        out_shape=[act] * 6 + [jax.ShapeDtypeStruct((SUBLANES, width), F32)],
        scratch_shapes=[pltpu.VMEM((tm, d), BF16),
                        pltpu.VMEM((tm + SUBLANES, COL_CHUNK), F32),
                        pltpu.VMEM((SUBLANES, width), F32)],
        compiler_params=pltpu.CompilerParams(dimension_semantics=("arbitrary",),
                                             vmem_limit_bytes=VMEM_LIMIT),
        name="in_proj",
    )(h, gain, w_in, conv_w, hist)


def _lam(lam_ref, lam_init):
    l = lam_ref[...]
    return (jnp.exp(jnp.sum(l[0:1] * l[1:2], axis=-1, keepdims=True))
            - jnp.exp(jnp.sum(l[2:3] * l[3:4], axis=-1, keepdims=True)) + lam_init)


def _subln(o, g, lam_init):
    return _rms(o, g) * (1.0 - lam_init)


def _key_features(pos0, n):
    pos = lax.broadcasted_iota(jnp.int32, (n, LANES), 0) + pos0
    lane = lax.broadcasted_iota(jnp.int32, (n, LANES), 1)
    hi = (pos >> 6).astype(F32)
    lo = (pos & (POS_SPLIT - 1)).astype(F32)
    feat = jnp.where(lane == LANE_POS_HI, hi, jnp.where(lane == LANE_POS_LO, lo, 0.0))
    return feat, lane


def _split_maps(x, feat, lane):
    first = jnp.where(lane < HEAD_DIM, x, feat)
    second = jnp.where(lane < HEAD_DIM, pltpu.roll(x, HEAD_DIM, axis=1), feat)
    return first, second


def _ones_rows(n):
    return jnp.where(lax.broadcasted_iota(jnp.int32, (ACC_ROWS - V_DIM, n), 0) == 0, 1.0, 0.0)


def _attn_body(q_ref, k_ref, v_ref, mk_ref, mv_ref, slope_ref, lam_ref, g_ref, o_ref,
               qt1_ref, qt2_ref, ka1_ref, ka2_ref, vt_ref, mask_ref,
               m01_ref, m02_ref, acc01_ref, acc02_ref,
               s1_ref, s2_ref, p1_ref, p2_ref, al1_ref, al2_ref, acc1_ref, acc2_ref, *, lam_init):
    n_q, _, tq = qt1_ref.shape
    n_k, tk, _ = ka1_ref.shape
    ratio = tq // tk
    n_items = ratio * n_q * (n_q + 1) // 2

    @pl.when((pl.program_id(0) == 0) & (pl.program_id(1) == 0))
    def _build_masks():
        kk = lax.broadcasted_iota(jnp.int32, (tk, tq), 0)
        qq = lax.broadcasted_iota(jnp.int32, (tk, tq), 1)
        for d in range(ratio):
            mask_ref[d] = jnp.where(kk + d * tk <= qq, 0.0, NEG)

    def key_tile(t, carry):
        r0 = pl.multiple_of(t * tk, tk)
        feat, lane = _key_features(N_META + r0, tk)
        k1, k2 = _split_maps(k_ref[pl.ds(r0, tk), :].astype(F32), feat, lane)
        ka1_ref[t] = k1.astype(BF16)
        ka2_ref[t] = k2.astype(BF16)
        vt_ref[t, 0:V_DIM, :] = v_ref[pl.ds(r0, tk), :].astype(F32).T.astype(BF16)
        vt_ref[t, V_DIM:ACC_ROWS, :] = _ones_rows(tk).astype(BF16)
        return carry

    lax.fori_loop(0, n_k, key_tile, 0)

    feat, lane = _key_features(0, N_META)
    mk1, mk2 = (a.astype(BF16) for a in _split_maps(mk_ref[...].astype(F32), feat, lane))
    mv_pad = jnp.concatenate([mv_ref[...].astype(F32), jnp.zeros((LANES - N_META, LANES), F32)], axis=0)
    mvt = jnp.concatenate([mv_pad.T[:, 0:N_META], _ones_rows(N_META)], axis=0).astype(BF16)

    slope = slope_ref[...]

    def query_tile(i, carry):
        r0 = pl.multiple_of(i * tq, tq)
        lane = lax.broadcasted_iota(jnp.int32, (tq, LANES), 1)
        qfeat = jnp.where(lane == LANE_POS_HI, slope * POS_SPLIT,
                          jnp.where(lane == LANE_POS_LO, slope, 0.0))
        q1, q2 = _split_maps(q_ref[pl.ds(r0, tq), :].astype(F32), qfeat, lane)
        for qm, mk, qt_ref, m0_ref, acc0_ref in ((q1, mk1, qt1_ref, m01_ref, acc01_ref),
                                                 (q2, mk2, qt2_ref, m02_ref, acc02_ref)):
            qt = qm.T.astype(BF16)
            qt_ref[i] = qt
            s_meta = _dot(mk, qt)
            m0 = jnp.max(s_meta, axis=0, keepdims=True)
            m0_ref[i] = m0
            acc0_ref[i] = _dot(mvt, jnp.exp(s_meta - m0).astype(BF16))
        return carry

    lax.fori_loop(0, n_q, query_tile, 0)

    acc1_ref[...] = acc01_ref[0]
    acc2_ref[...] = acc02_ref[0]
    p1_ref[1] = jnp.zeros((tk, tq), BF16)
    p2_ref[1] = jnp.zeros((tk, tq), BF16)
    al1_ref[1] = jnp.ones((1, tq), F32)
    al2_ref[1] = jnp.ones((1, tq), F32)

    def stage_a(qi, t, slot):
        s1_ref[slot] = _dot(ka1_ref[t], qt1_ref[qi])
        s2_ref[slot] = _dot(ka2_ref[t], qt2_ref[qi])

    def stage_b(qi, t, slot, ms, masked):
        first = t == 0
        out = []
        for s_ref, p_ref, al_ref, m0_ref, m in ((s1_ref, p1_ref, al1_ref, m01_ref, ms[0]),
                                                (s2_ref, p2_ref, al2_ref, m02_ref, ms[1])):
            s = s_ref[slot]
            if masked:
                s = s + mask_ref[jnp.maximum(t - qi * ratio, 0)]
            m_old = jnp.where(first, m0_ref[qi], m)
            m_new = jnp.maximum(m_old, jnp.max(s, axis=0, keepdims=True))
            al_ref[slot] = jnp.exp(m_old - m_new)
            p_ref[slot] = jnp.exp(s - m_new).astype(BF16)
            out.append(m_new)
        return tuple(out)

    def stage_c(t, slot):
        for acc_ref, p_ref, al_ref in ((acc1_ref, p1_ref, al1_ref), (acc2_ref, p2_ref, al2_ref)):
            acc_ref[...] = al_ref[slot] * acc_ref[...] + _dot(vt_ref[t], p_ref[slot])

    def advance(qi, t):
        end = t + 1 == (qi + 1) * ratio
        more = qi + 1 < n_q
        return (jnp.where(end & more, qi + 1, qi),
                jnp.where(end, jnp.where(more, 0, t), t + 1))

    def iteration(w, slot_b, carry):
        qa, ta, qb, tb, qc, tc, m1, m2 = carry
        slot_ac = 1 - slot_b

        def run(masked):
            def f():
                stage_a(qa, ta, slot_ac)
                ms = stage_b(qb, tb, slot_b, (m1, m2), masked)
                stage_c(tc, slot_ac)
                return ms
            return f

        m1, m2 = lax.cond(tb >= qb * ratio, run(True), run(False))

        @pl.when((w >= 1) & (w <= n_items) & (tc + 1 == (qc + 1) * ratio))
        def _finish_query_tile():
            a1 = acc1_ref[...]
            a2 = acc2_ref[...]
            o_t = (a1[0:V_DIM] / a1[V_DIM:V_DIM + 1]
                   - _lam(lam_ref, lam_init) * (a2[0:V_DIM] / a2[V_DIM:V_DIM + 1]))
            r0 = pl.multiple_of(qc * tq, tq)
            o_ref[pl.ds(r0, tq), :] = _subln(o_t.T, g_ref[...], lam_init).astype(BF16)
            nxt = jnp.minimum(qc + 1, n_q - 1)
            acc1_ref[...] = acc01_ref[nxt]
            acc2_ref[...] = acc02_ref[nxt]

        qn, tn = advance(qa, ta)
        return qn, tn, qa, ta, qb, tb, m1, m2

    zero = jnp.int32(0)
    stage_a(zero, zero, 0)
    q1_, t1_ = advance(zero, zero)
    m_init = jnp.full((1, tq), NEG, F32)

    def iteration_pair(i, carry):
        return iteration(2 * i + 1, 1, iteration(2 * i, 0, carry))

    lax.fori_loop(0, (n_items + 2) // 2, iteration_pair,
                  (q1_, t1_, zero, zero, zero, zero, m_init, m_init))


def _attention(q, k, v, mk, mv, slopes, lam_vecs, g, *, batch, seq, lam_init):
    rows, width = q.shape
    tq, tk = min(Q_TILE, seq), min(K_TILE, seq)
    n_q, n_k = seq // tq, seq // tk
    assert seq % tq == 0 and tq % tk == 0 and tk % LANES == 0 and width == N_HEADS * V_DIM
    full = lambda a: pl.BlockSpec(a.shape, lambda b, h: (0,) * a.ndim)
    seq_spec = pl.BlockSpec((seq, V_DIM), lambda b, h: (b, h))
    meta_spec = pl.BlockSpec((N_META, V_DIM), lambda b, h: (0, h))
    vmem = pltpu.VMEM
    return pl.pallas_call(
        functools.partial(_attn_body, lam_init=lam_init),
        grid=(batch, N_HEADS),
        in_specs=[seq_spec, seq_spec, seq_spec, meta_spec, meta_spec,
                  pl.BlockSpec((None, 1, LANES), lambda b, h: (h, 0, 0)), full(lam_vecs), full(g)],
        out_specs=seq_spec,
        out_shape=jax.ShapeDtypeStruct((rows, width), BF16),
        scratch_shapes=[vmem((n_q, LANES, tq), BF16), vmem((n_q, LANES, tq), BF16),
                        vmem((n_k, tk, LANES), BF16), vmem((n_k, tk, LANES), BF16),
                        vmem((n_k, ACC_ROWS, tk), BF16),
                        vmem((tq // tk, tk, tq), F32),
                        vmem((n_q, 1, tq), F32), vmem((n_q, 1, tq), F32),
                        vmem((n_q, ACC_ROWS, tq), F32), vmem((n_q, ACC_ROWS, tq), F32),
                        vmem((2, tk, tq), F32), vmem((2, tk, tq), F32),
                        vmem((2, tk, tq), BF16), vmem((2, tk, tq), BF16),
                        vmem((2, 1, tq), F32), vmem((2, 1, tq), F32),
                        vmem((ACC_ROWS, tq), F32), vmem((ACC_ROWS, tq), F32)],
        compiler_params=pltpu.CompilerParams(dimension_semantics=("arbitrary", "arbitrary"),
                                             vmem_limit_bytes=VMEM_LIMIT),
        name="diff_attention",
    )(q, k, v, mk, mv, slopes, lam_vecs, g)


def _meta_attn_body(q_ref, k_ref, v_ref, slope_ref, lam_ref, g_ref, o_ref, *, lam_init):
    q = q_ref[...]
    k = k_ref[...]
    n = q.shape[0]
    lane = lax.broadcasted_iota(jnp.int32, q.shape, 1)
    zero = jnp.zeros_like(q)
    contract_last = (((1,), (1,)), ((), ()))
    s1 = lax.dot_general(jnp.where(lane < HEAD_DIM, q, zero), k, contract_last,
                         preferred_element_type=F32)
    s2 = lax.dot_general(jnp.where(lane >= HEAD_DIM, q, zero), k, contract_last,
                         preferred_element_type=F32)
    qpos = lax.broadcasted_iota(jnp.int32, (n, n), 0)
    kpos = lax.broadcasted_iota(jnp.int32, (n, n), 1)
    bias = -slope_ref[...][:, 0:1] * (qpos - kpos).astype(F32)

    def softmax(s):
        s = jnp.where(kpos <= qpos, s + bias, NEG)
        e = jnp.exp(s - jnp.max(s, axis=-1, keepdims=True))
        return e / jnp.sum(e, axis=-1, keepdims=True)

    w = softmax(s1) - _lam(lam_ref, lam_init) * softmax(s2)
    o_ref[...] = _subln(_dot(w.astype(BF16), v_ref[...]), g_ref[...], lam_init).astype(BF16)


def _meta_attention(q, k, v, slopes, lam_vecs, g, *, lam_init):
    full = lambda a: pl.BlockSpec(a.shape, lambda h: (0,) * a.ndim)
    head = pl.BlockSpec((N_META, V_DIM), lambda h: (0, h))
    return pl.pallas_call(
        functools.partial(_meta_attn_body, lam_init=lam_init),
        grid=(N_HEADS,),
        in_specs=[head, head, head, pl.BlockSpec((None, 1, LANES), lambda h: (h, 0, 0)),
                  full(lam_vecs), full(g)],
        out_specs=head,
        out_shape=jax.ShapeDtypeStruct(q.shape, BF16),
        name="meta_attention",
    )(q, k, v, slopes, lam_vecs, g)


def _merge_body(ypre_ref, o_ref, sga_ref, sgb_ref, h_ref, wc_ref, wa_ref, wm_ref, g_ref, out_ref):
    ya = _dot(ypre_ref[...], wc_ref[...])
    yb = _dot(o_ref[...], wa_ref[...])
    gated = sga_ref[...].astype(F32) * ya + sgb_ref[...].astype(F32) * yb
    mix = _dot(gated.astype(BF16), wm_ref[...])
    out_ref[...] = h_ref[...] + _rms(mix, g_ref[...])


def _merge(ypre, o, sga, sgb, h, wc, wa, wm, g):
    rows, d = h.shape
    tm = min(ROW_TILE, rows)
    assert rows % tm == 0
    row_spec = lambda n: pl.BlockSpec((tm, n), lambda i: (i, 0))
    return pl.pallas_call(
        _merge_body,
        grid=(rows // tm,),
        in_specs=[row_spec(ypre.shape[1]), row_spec(o.shape[1]), row_spec(sga.shape[1]),
                  row_spec(sgb.shape[1]), row_spec(d),
                  _resident(wc.shape), _resident(wa.shape), _resident(wm.shape), _resident(g.shape)],
        out_specs=row_spec(d),
        out_shape=jax.ShapeDtypeStruct((rows, d), F32),
        compiler_params=pltpu.CompilerParams(dimension_semantics=("arbitrary",),
                                             vmem_limit_bytes=VMEM_LIMIT),
        name="merge",
    )(ypre, o, sga, sgb, h, wc, wa, wm, g)


def _ffn_body(h_ref, gpre_ref, wup_ref, cw_ref, cb_ref, hist_ref, wdn_ref, gpost_ref,
              out_ref, tail_ref, f_ref, buf_ref, carry_ref, acc_ref, *, tiles_per_seq, d_ff):
    @pl.when(pl.program_id(0) % tiles_per_seq == 0)
    def _():
        carry_ref[...] = hist_ref[...]

    f_ref[...] = _rms(h_ref[...], gpre_ref[...]).astype(BF16)

    for c in range(d_ff // COL_CHUNK):
        halves = []
        for half in range(2):
            lo = half * d_ff + c * COL_CHUNK
            cols = slice(lo, lo + COL_CHUNK)
            z = _dot(f_ref[...], wup_ref[:, cols])
            halves.append(_causal_conv3(z, cw_ref[:, cols], buf_ref, carry_ref, cols) + cb_ref[:, cols])
        a = (jax.nn.silu(halves[0]) * halves[1]).astype(BF16)
        y = _dot(a, wdn_ref[c * COL_CHUNK:(c + 1) * COL_CHUNK, :])
        if c == 0:
            acc_ref[...] = y
        else:
            acc_ref[...] += y

    tail_ref[...] = carry_ref[...]
    out_ref[...] = h_ref[...] + _rms(acc_ref[...], gpost_ref[...])


def _ffn(h, gpre, wup, conv_w, conv_b, hist, wdn, gpost, *, seq_rows):
    rows, d = h.shape
    d_ff = wdn.shape[0]
    tm = min(ROW_TILE, rows)
    assert rows % tm == 0 and seq_rows % tm == 0 and d_ff % COL_CHUNK == 0
    row_spec = pl.BlockSpec((tm, d), lambda i: (i, 0))
    return pl.pallas_call(
        functools.partial(_ffn_body, tiles_per_seq=seq_rows // tm, d_ff=d_ff),
        grid=(rows // tm,),
        in_specs=[row_spec, _resident(gpre.shape), _resident(wup.shape), _resident(conv_w.shape),
                  _resident(conv_b.shape), _resident(hist.shape), _resident(wdn.shape),
                  _resident(gpost.shape)],
        out_specs=[row_spec, pl.BlockSpec((SUBLANES, 2 * d_ff), lambda i: (0, 0))],
        out_shape=[jax.ShapeDtypeStruct((rows, d), F32),
                   jax.ShapeDtypeStruct((SUBLANES, 2 * d_ff), F32)],
        scratch_shapes=[pltpu.VMEM((tm, d), BF16),
                        pltpu.VMEM((tm + SUBLANES, COL_CHUNK), F32),
                        pltpu.VMEM((SUBLANES, 2 * d_ff), F32),
                        pltpu.VMEM((tm, d), F32)],
        compiler_params=pltpu.CompilerParams(dimension_semantics=("arbitrary",),
                                             vmem_limit_bytes=VMEM_LIMIT),
        name="ffn",
    )(h, gpre, wup, conv_w, conv_b, hist, wdn, gpost)


def _alibi_slopes():
    slopes = 2.0 ** (-8.0 * np.arange(1, N_HEADS + 1, dtype=np.float64) / N_HEADS)
    assert np.array_equal(slopes.astype(BF16).astype(np.float64), slopes), "slopes must be exact in bf16"
    return jnp.asarray(np.broadcast_to(slopes[:, None, None], (N_HEADS, 1, LANES)), F32)


def kernel(x, meta_tokens, w_in, conv_w, w_conv_out, lambda_q1, lambda_k1, lambda_q2, lambda_k2,
           subln_g, w_attn_out, w_mix_out, norm_mix_pre, norm_mix_post, w_ffn_up, ffn_conv_w,
           ffn_conv_b, w_ffn_down, norm_ffn_pre, norm_ffn_post):
    batch, seq, d = x.shape
    depth = w_in.shape[0]
    assert meta_tokens.shape[0] == N_META and seq + N_META <= POS_SPLIT * 256
    slopes = _alibi_slopes()
    row = lambda a: a.reshape(1, -1)

    hx = x.reshape(batch * seq, d)
    hm = meta_tokens.astype(x.dtype)
    for l in range(depth):
        lam_init = 0.8 - 0.6 * math.exp(-0.3 * l)
        w_in_l = w_in[l].astype(BF16)
        lam_vecs = jnp.stack([lambda_q1[l], lambda_k1[l], lambda_q2[l], lambda_k2[l]]).astype(F32)
        g_sub = row(subln_g[l])
        zero_hist = jnp.zeros((SUBLANES, conv_w.shape[2]), F32)

        ypre_m, q_m, k_m, v_m, sga_m, sgb_m, u_tail = _in_proj(
            hm, row(norm_mix_pre[l]), w_in_l, conv_w[l], zero_hist, seq_rows=N_META)
        ypre, q, k, v, sga, sgb, _ = _in_proj(
            hx, row(norm_mix_pre[l]), w_in_l, conv_w[l], u_tail, seq_rows=seq)

        o_m = _meta_attention(q_m, k_m, v_m, slopes, lam_vecs, g_sub, lam_init=lam_init)
        o = _attention(q, k, v, k_m, v_m, slopes, lam_vecs, g_sub,
                       batch=batch, seq=seq, lam_init=lam_init)

        wc, wa, wm = (w.astype(BF16) for w in (w_conv_out[l], w_attn_out[l], w_mix_out[l]))
        hm = _merge(ypre_m, o_m, sga_m, sgb_m, hm, wc, wa, wm, row(norm_mix_post[l]))
        hx = _merge(ypre, o, sga, sgb, hx, wc, wa, wm, row(norm_mix_post[l]))

        wup, wdn = w_ffn_up[l].astype(BF16), w_ffn_down[l].astype(BF16)
        zero_hist = jnp.zeros((SUBLANES, wup.shape[1]), F32)
        ffn = functools.partial(_ffn, gpre=row(norm_ffn_pre[l]), wup=wup, conv_w=ffn_conv_w[l],
                                conv_b=row(ffn_conv_b[l]), wdn=wdn, gpost=row(norm_ffn_post[l]))
        hm, z_tail = ffn(hm, hist=zero_hist, seq_rows=N_META)
        hx, _ = ffn(hx, hist=z_tail, seq_rows=seq)
    return hx.reshape(batch, seq, d)
```

```python
import functools
import math

import numpy as np
import jax
import jax.numpy as jnp
from jax import lax
from jax.experimental import pallas as pl
from jax.experimental.pallas import tpu as pltpu

F32 = jnp.float32
BF16 = jnp.bfloat16

N_META = 16
N_HEADS = 8
HEAD_DIM = 64
V_DIM = 2 * HEAD_DIM
N_SEG = 8
RMS_EPS = 1e-6
NEG = -1e30

LANES = 128
SUBLANES = 8
ROW_TILE = 512
COL_CHUNK = 256
Q_TILE = 512
K_TILE = 512
ACC_ROWS = V_DIM + 16
VMEM_LIMIT = 56 * 1024 * 1024

LANE_POS_HI = HEAD_DIM
LANE_POS_LO = HEAD_DIM + 1
POS_SPLIT = 64


def _rms(x, g):
    ms = jnp.mean(x * x, axis=-1, keepdims=True)
    return x * lax.rsqrt(ms + RMS_EPS) * g


def _dot(a, b):
    return jnp.dot(a, b, preferred_element_type=F32)


def _causal_conv3(z, w, buf_ref, carry_ref, cols):
    tm = z.shape[0]
    buf_ref[0:SUBLANES, :] = carry_ref[:, cols]
    buf_ref[SUBLANES:SUBLANES + tm, :] = z
    y = (w[2:3] * z + w[1:2] * buf_ref[SUBLANES - 1:SUBLANES - 1 + tm, :]
         + w[0:1] * buf_ref[SUBLANES - 2:SUBLANES - 2 + tm, :])
    carry_ref[:, cols] = buf_ref[tm:tm + SUBLANES, :]
    return y


def _in_proj_body(x_ref, g_ref, w_ref, cw_ref, hist_ref,
                  ypre_ref, q_ref, k_ref, v_ref, sga_ref, sgb_ref, tail_ref,
                  xn_ref, buf_ref, carry_ref, *, tiles_per_seq, width, q_scale):
    @pl.when(pl.program_id(0) % tiles_per_seq == 0)
    def _():
        carry_ref[...] = hist_ref[...]

    xn_ref[...] = _rms(x_ref[...], g_ref[...]).astype(BF16)

    for c in range(width // COL_CHUNK):
        cols = slice(c * COL_CHUNK, (c + 1) * COL_CHUNK)

        def proj(seg):
            lo = seg * width + c * COL_CHUNK
            return _dot(xn_ref[...], w_ref[:, lo:lo + COL_CHUNK])

        u = proj(1) * proj(2)
        y = _causal_conv3(u, cw_ref[:, cols], buf_ref, carry_ref, cols)
        ypre_ref[:, cols] = (proj(0) * y).astype(BF16)
        q_ref[:, cols] = (proj(3) * q_scale).astype(BF16)
        k_ref[:, cols] = proj(4).astype(BF16)
        v_ref[:, cols] = proj(5).astype(BF16)
        sga_ref[:, cols] = jax.nn.sigmoid(proj(6)).astype(BF16)
        sgb_ref[:, cols] = jax.nn.sigmoid(proj(7)).astype(BF16)

    tail_ref[...] = carry_ref[...]


def _resident(shape):
    return pl.BlockSpec(shape, lambda i: (0,) * len(shape), pipeline_mode=pl.Buffered(1))


def _in_proj(h, gain, w_in, conv_w, hist, *, seq_rows):
    rows, d = h.shape
    width = w_in.shape[1] // N_SEG
    tm = min(ROW_TILE, rows)
    assert rows % tm == 0 and seq_rows % tm == 0 and width % COL_CHUNK == 0
    row_spec = lambda n: pl.BlockSpec((tm, n), lambda i: (i, 0))
    act = jax.ShapeDtypeStruct((rows, width), BF16)
    return pl.pallas_call(
        functools.partial(_in_proj_body, tiles_per_seq=seq_rows // tm, width=width,
                          q_scale=HEAD_DIM ** -0.5),
        grid=(rows // tm,),
        in_specs=[row_spec(d), _resident((1, d)), _resident(w_in.shape), _resident(conv_w.shape),
                  _resident(hist.shape)],
        out_specs=[row_spec(width)] * 6 + [pl.BlockSpec((SUBLANES, width), lambda i: (0, 0))],
        out_shape=[act] * 6 + [jax.ShapeDtypeStruct((SUBLANES, width), F32)],
        scratch_shapes=[pltpu.VMEM((tm, d), BF16),
                        pltpu.VMEM((tm + SUBLANES, COL_CHUNK), F32),
                        pltpu.VMEM((SUBLANES, width), F32)],
        compiler_params=pltpu.CompilerParams(dimension_semantics=("arbitrary",),
                                             vmem_limit_bytes=VMEM_LIMIT),
        name="in_proj",
    )(h, gain, w_in, conv_w, hist)


def _lam(lam_ref, lam_init):
    l = lam_ref[...]
    return (jnp.exp(jnp.sum(l[0:1] * l[1:2], axis=-1, keepdims=True))
            - jnp.exp(jnp.sum(l[2:3] * l[3:4], axis=-1, keepdims=True)) + lam_init)


def _subln(o, g, lam_init):
    return _rms(o, g) * (1.0 - lam_init)


def _key_features(pos0, n):
    pos = lax.broadcasted_iota(jnp.int32, (n, LANES), 0) + pos0
    lane = lax.broadcasted_iota(jnp.int32, (n, LANES), 1)
    hi = (pos >> 6).astype(F32)
    lo = (pos & (POS_SPLIT - 1)).astype(F32)
    feat = jnp.where(lane == LANE_POS_HI, hi, jnp.where(lane == LANE_POS_LO, lo, 0.0))
    return feat, lane


def _split_maps(x, feat, lane):
    first = jnp.where(lane < HEAD_DIM, x, feat)
    second = jnp.where(lane < HEAD_DIM, pltpu.roll(x, HEAD_DIM, axis=1), feat)
    return first, second


def _ones_rows(n):
    return jnp.where(lax.broadcasted_iota(jnp.int32, (ACC_ROWS - V_DIM, n), 0) == 0, 1.0, 0.0)


def _attn_body(q_ref, k_ref, v_ref, mk_ref, mv_ref, slope_ref, lam_ref, g_ref, o_ref,
               qt1_ref, qt2_ref, ka1_ref, ka2_ref, vt_ref, mask_ref,
               m01_ref, m02_ref, acc01_ref, acc02_ref,
               s1a_ref, s1b_ref, s2a_ref, s2b_ref, tm1a_ref, tm1b_ref, tm2a_ref, tm2b_ref,
               acc1_ref, acc2_ref, *, lam_init):
    n_q, _, tq = qt1_ref.shape
    n_k, tk, _ = ka1_ref.shape
    s1_ref, s2_ref = (s1a_ref, s1b_ref), (s2a_ref, s2b_ref)
    tmax1_ref, tmax2_ref = (tm1a_ref, tm1b_ref), (tm2a_ref, tm2b_ref)
    ratio = tq // tk
    n_items = ratio * n_q * (n_q + 1) // 2

    @pl.when((pl.program_id(0) == 0) & (pl.program_id(1) == 0))
    def _build_masks():
        kk = lax.broadcasted_iota(jnp.int32, (tk, tq), 0)
        qq = lax.broadcasted_iota(jnp.int32, (tk, tq), 1)
        for d in range(ratio):
            mask_ref[d] = jnp.where(kk + d * tk <= qq, 0.0, NEG)

    def key_tile(t, carry):
        r0 = pl.multiple_of(t * tk, tk)
        feat, lane = _key_features(N_META + r0, tk)
        k1, k2 = _split_maps(k_ref[pl.ds(r0, tk), :].astype(F32), feat, lane)
        ka1_ref[t] = k1.astype(BF16)
        ka2_ref[t] = k2.astype(BF16)
        vt_ref[t, 0:V_DIM, :] = v_ref[pl.ds(r0, tk), :].astype(F32).T.astype(BF16)
        vt_ref[t, V_DIM:ACC_ROWS, :] = _ones_rows(tk).astype(BF16)
        return carry

    lax.fori_loop(0, n_k, key_tile, 0)

    feat, lane = _key_features(0, N_META)
    mk1, mk2 = (a.astype(BF16) for a in _split_maps(mk_ref[...].astype(F32), feat, lane))
    mv_pad = jnp.concatenate([mv_ref[...].astype(F32), jnp.zeros((LANES - N_META, LANES), F32)], axis=0)
    mvt = jnp.concatenate([mv_pad.T[:, 0:N_META], _ones_rows(N_META)], axis=0).astype(BF16)

    slope = slope_ref[...]

    def query_tile(i, carry):
        r0 = i * tq
        lane = lax.broadcasted_iota(jnp.int32, (tq, LANES), 1)
        qfeat = jnp.where(lane == LANE_POS_HI, slope * POS_SPLIT,
                          jnp.where(lane == LANE_POS_LO, slope, 0.0))
        q1, q2 = _split_maps(q_ref[pl.ds(r0, tq), :].astype(F32), qfeat, lane)
        for qm, mk, qt_ref, m0_ref, acc0_ref in ((q1, mk1, qt1_ref, m01_ref, acc01_ref),
                                                 (q2, mk2, qt2_ref, m02_ref, acc02_ref)):
            qt = qm.T.astype(BF16)
            qt_ref[i] = qt
            s_meta = _dot(mk, qt)
            m0 = jnp.max(s_meta, axis=0, keepdims=True)
            m0_ref[i] = m0
            acc0_ref[i] = _dot(mvt, jnp.exp(s_meta - m0).astype(BF16))
        return carry

    for i in range(n_q):
        query_tile(i, 0)

    acc1_ref[...] = acc01_ref[0]
    acc2_ref[...] = acc02_ref[0]

    def stage_a(qi, t, slot, masked):
        for ka_ref, qt_ref, s_ref, tmax_ref in ((ka1_ref, qt1_ref, s1_ref, tmax1_ref),
                                                (ka2_ref, qt2_ref, s2_ref, tmax2_ref)):
            s = _dot(ka_ref[t], qt_ref[qi])
            if masked:
                s = s + mask_ref[jnp.maximum(t - qi * ratio, 0)]
            s_ref[slot][...] = s
            tmax_ref[slot][...] = jnp.max(s, axis=0, keepdims=True)

    def stage_b(qi, t, slot, ms):
        first = t == 0
        out = []
        for s_ref, tmax_ref, acc_ref, m0_ref, m in ((s1_ref, tmax1_ref, acc1_ref, m01_ref, ms[0]),
                                                    (s2_ref, tmax2_ref, acc2_ref, m02_ref, ms[1])):
            m_old = jnp.where(first, m0_ref[qi], m)
            m_new = jnp.maximum(m_old, tmax_ref[slot][...])
            p = jnp.exp(s_ref[slot][...] - m_new).astype(BF16)
            acc_ref[...] = jnp.exp(m_old - m_new) * acc_ref[...] + _dot(vt_ref[t], p)
            out.append(m_new)
        return tuple(out)

    def advance(qi, t):
        end = t + 1 == (qi + 1) * ratio
        more = qi + 1 < n_q
        return (jnp.where(end & more, qi + 1, qi),
                jnp.where(end, jnp.where(more, 0, t), t + 1))

    def iteration(w, slot_b, carry):
        qa, ta, qb, tb, m1, m2 = carry

        def run(masked):
            def f():
                stage_a(qa, ta, 1 - slot_b, masked)
                return stage_b(qb, tb, slot_b, (m1, m2))
            return f

        m1, m2 = lax.cond(ta >= qa * ratio, run(True), run(False))

        @pl.when((w < n_items) & (tb + 1 == (qb + 1) * ratio))
        def _finish_query_tile():
            w1 = 1.0 / acc1_ref[V_DIM:V_DIM + 1, :]
            w2 = _lam(lam_ref, lam_init) / acc2_ref[V_DIM:V_DIM + 1, :]
            o_t = acc1_ref[0:V_DIM, :] * w1 - acc2_ref[0:V_DIM, :] * w2
            scale = lax.rsqrt(jnp.mean(o_t * o_t, axis=0, keepdims=True) + RMS_EPS) * (1.0 - lam_init)
            r0 = pl.multiple_of(qb * tq, tq)
            o_ref[pl.ds(r0, tq), :] = ((o_t * scale).T * g_ref[...]).astype(BF16)
            nxt = jnp.minimum(qb + 1, n_q - 1)
            acc1_ref[...] = acc01_ref[nxt]
            acc2_ref[...] = acc02_ref[nxt]

        qn, tn = advance(qa, ta)
        return qn, tn, qa, ta, m1, m2

    zero = jnp.int32(0)
    stage_a(zero, zero, 0, True)
    q1_, t1_ = advance(zero, zero)
    m_init = jnp.full((1, tq), NEG, F32)

    def iteration_pair(i, carry):
        return iteration(2 * i + 1, 1, iteration(2 * i, 0, carry))

    lax.fori_loop(0, (n_items + 1) // 2, iteration_pair, (q1_, t1_, zero, zero, m_init, m_init))


def _attention(q, k, v, mk, mv, slopes, lam_vecs, g, *, batch, seq, lam_init):
    rows, width = q.shape
    tq, tk = min(Q_TILE, seq), min(K_TILE, seq)
    n_q, n_k = seq // tq, seq // tk
    assert seq % tq == 0 and tq % tk == 0 and tk % LANES == 0 and width == N_HEADS * V_DIM
    full = lambda a: pl.BlockSpec(a.shape, lambda b, h: (0,) * a.ndim)
    seq_spec = pl.BlockSpec((seq, V_DIM), lambda b, h: (b, h))
    meta_spec = pl.BlockSpec((N_META, V_DIM), lambda b, h: (0, h))
    vmem = pltpu.VMEM
    return pl.pallas_call(
        functools.partial(_attn_body, lam_init=lam_init),
        grid=(batch, N_HEADS),
        in_specs=[seq_spec, seq_spec, seq_spec, meta_spec, meta_spec,
                  pl.BlockSpec((None, 1, LANES), lambda b, h: (h, 0, 0)), full(lam_vecs), full(g)],
        out_specs=seq_spec,
        out_shape=jax.ShapeDtypeStruct((rows, width), BF16),
        scratch_shapes=[vmem((n_q, LANES, tq), BF16), vmem((n_q, LANES, tq), BF16),
                        vmem((n_k, tk, LANES), BF16), vmem((n_k, tk, LANES), BF16),
                        vmem((n_k, ACC_ROWS, tk), BF16),
                        vmem((tq // tk, tk, tq), F32),
                        vmem((n_q, 1, tq), F32), vmem((n_q, 1, tq), F32),
                        vmem((n_q, ACC_ROWS, tq), F32), vmem((n_q, ACC_ROWS, tq), F32),
                        vmem((tk, tq), F32), vmem((tk, tq), F32), vmem((tk, tq), F32), vmem((tk, tq), F32),
                        vmem((1, tq), F32), vmem((1, tq), F32), vmem((1, tq), F32), vmem((1, tq), F32),
                        vmem((ACC_ROWS, tq), F32), vmem((ACC_ROWS, tq), F32)],
        compiler_params=pltpu.CompilerParams(dimension_semantics=("arbitrary", "arbitrary"),
                                             vmem_limit_bytes=VMEM_LIMIT),
        name="diff_attention",
    )(q, k, v, mk, mv, slopes, lam_vecs, g)


def _meta_attn_body(q_ref, k_ref, v_ref, slope_ref, lam_ref, g_ref, o_ref, *, lam_init):
    q = q_ref[...]
    k = k_ref[...]
    n = q.shape[0]
    lane = lax.broadcasted_iota(jnp.int32, q.shape, 1)
    zero = jnp.zeros_like(q)
    contract_last = (((1,), (1,)), ((), ()))
    s1 = lax.dot_general(jnp.where(lane < HEAD_DIM, q, zero), k, contract_last,
                         preferred_element_type=F32)
    s2 = lax.dot_general(jnp.where(lane >= HEAD_DIM, q, zero), k, contract_last,
                         preferred_element_type=F32)
    qpos = lax.broadcasted_iota(jnp.int32, (n, n), 0)
    kpos = lax.broadcasted_iota(jnp.int32, (n, n), 1)
    bias = -slope_ref[...][:, 0:1] * (qpos - kpos).astype(F32)

    def softmax(s):
        s = jnp.where(kpos <= qpos, s + bias, NEG)
        e = jnp.exp(s - jnp.max(s, axis=-1, keepdims=True))
        return e / jnp.sum(e, axis=-1, keepdims=True)

    w = softmax(s1) - _lam(lam_ref, lam_init) * softmax(s2)
    o_ref[...] = _subln(_dot(w.astype(BF16), v_ref[...]), g_ref[...], lam_init).astype(BF16)


def _meta_attention(q, k, v, slopes, lam_vecs, g, *, lam_init):
    full = lambda a: pl.BlockSpec(a.shape, lambda h: (0,) * a.ndim)
    head = pl.BlockSpec((N_META, V_DIM), lambda h: (0, h))
    return pl.pallas_call(
        functools.partial(_meta_attn_body, lam_init=lam_init),
        grid=(N_HEADS,),
        in_specs=[head, head, head, pl.BlockSpec((None, 1, LANES), lambda h: (h, 0, 0)),
                  full(lam_vecs), full(g)],
        out_specs=head,
        out_shape=jax.ShapeDtypeStruct(q.shape, BF16),
        name="meta_attention",
    )(q, k, v, slopes, lam_vecs, g)


def _merge_body(ypre_ref, o_ref, sga_ref, sgb_ref, h_ref, wc_ref, wa_ref, wm_ref, g_ref, out_ref):
    ya = _dot(ypre_ref[...], wc_ref[...])
    yb = _dot(o_ref[...], wa_ref[...])
    gated = sga_ref[...].astype(F32) * ya + sgb_ref[...].astype(F32) * yb
    mix = _dot(gated.astype(BF16), wm_ref[...])
    out_ref[...] = h_ref[...] + _rms(mix, g_ref[...])


def _merge(ypre, o, sga, sgb, h, wc, wa, wm, g):
    rows, d = h.shape
    tm = min(ROW_TILE, rows)
    assert rows % tm == 0
    row_spec = lambda n: pl.BlockSpec((tm, n), lambda i: (i, 0))
    return pl.pallas_call(
        _merge_body,
        grid=(rows // tm,),
        in_specs=[row_spec(ypre.shape[1]), row_spec(o.shape[1]), row_spec(sga.shape[1]),
                  row_spec(sgb.shape[1]), row_spec(d),
                  _resident(wc.shape), _resident(wa.shape), _resident(wm.shape), _resident(g.shape)],
        out_specs=row_spec(d),
        out_shape=jax.ShapeDtypeStruct((rows, d), F32),
        compiler_params=pltpu.CompilerParams(dimension_semantics=("arbitrary",),
                                             vmem_limit_bytes=VMEM_LIMIT),
        name="merge",
    )(ypre, o, sga, sgb, h, wc, wa, wm, g)


def _ffn_body(h_ref, gpre_ref, wup_ref, cw_ref, cb_ref, hist_ref, wdn_ref, gpost_ref,
              out_ref, tail_ref, f_ref, buf_ref, carry_ref, acc_ref, *, tiles_per_seq, d_ff):
    @pl.when(pl.program_id(0) % tiles_per_seq == 0)
    def _():
        carry_ref[...] = hist_ref[...]

    f_ref[...] = _rms(h_ref[...], gpre_ref[...]).astype(BF16)

    for c in range(d_ff // COL_CHUNK):
        halves = []
        for half in range(2):
            lo = half * d_ff + c * COL_CHUNK
            cols = slice(lo, lo + COL_CHUNK)
            z = _dot(f_ref[...], wup_ref[:, cols])
            halves.append(_causal_conv3(z, cw_ref[:, cols], buf_ref, carry_ref, cols) + cb_ref[:, cols])
        a = (jax.nn.silu(halves[0]) * halves[1]).astype(BF16)
        y = _dot(a, wdn_ref[c * COL_CHUNK:(c + 1) * COL_CHUNK, :])
        if c == 0:
            acc_ref[...] = y
        else:
            acc_ref[...] += y

    tail_ref[...] = carry_ref[...]
    out_ref[...] = h_ref[...] + _rms(acc_ref[...], gpost_ref[...])


def _ffn(h, gpre, wup, conv_w, conv_b, hist, wdn, gpost, *, seq_rows):
    rows, d = h.shape
    d_ff = wdn.shape[0]
    tm = min(ROW_TILE, rows)
    assert rows % tm == 0 and seq_rows % tm == 0 and d_ff % COL_CHUNK == 0
    row_spec = pl.BlockSpec((tm, d), lambda i: (i, 0))
    return pl.pallas_call(
        functools.partial(_ffn_body, tiles_per_seq=seq_rows // tm, d_ff=d_ff),
        grid=(rows // tm,),
        in_specs=[row_spec, _resident(gpre.shape), _resident(wup.shape), _resident(conv_w.shape),
                  _resident(conv_b.shape), _resident(hist.shape), _resident(wdn.shape),
                  _resident(gpost.shape)],
        out_specs=[row_spec, pl.BlockSpec((SUBLANES, 2 * d_ff), lambda i: (0, 0))],
        out_shape=[jax.ShapeDtypeStruct((rows, d), F32),
                   jax.ShapeDtypeStruct((SUBLANES, 2 * d_ff), F32)],
        scratch_shapes=[pltpu.VMEM((tm, d), BF16),
                        pltpu.VMEM((tm + SUBLANES, COL_CHUNK), F32),
                        pltpu.VMEM((SUBLANES, 2 * d_ff), F32),
                        pltpu.VMEM((tm, d), F32)],
        compiler_params=pltpu.CompilerParams(dimension_semantics=("arbitrary",),
                                             vmem_limit_bytes=VMEM_LIMIT),
        name="ffn",
    )(h, gpre, wup, conv_w, conv_b, hist, wdn, gpost)


def _alibi_slopes():
    slopes = 2.0 ** (-8.0 * np.arange(1, N_HEADS + 1, dtype=np.float64) / N_HEADS)
    assert np.array_equal(slopes.astype(BF16).astype(np.float64), slopes), "slopes must be exact in bf16"
    return jnp.asarray(np.broadcast_to(slopes[:, None, None], (N_HEADS, 1, LANES)), F32)


def kernel(x, meta_tokens, w_in, conv_w, w_conv_out, lambda_q1, lambda_k1, lambda_q2, lambda_k2,
           subln_g, w_attn_out, w_mix_out, norm_mix_pre, norm_mix_post, w_ffn_up, ffn_conv_w,
           ffn_conv_b, w_ffn_down, norm_ffn_pre, norm_ffn_post):
    batch, seq, d = x.shape
    depth = w_in.shape[0]
    assert meta_tokens.shape[0] == N_META and seq + N_META <= POS_SPLIT * 256
    slopes = _alibi_slopes()
    row = lambda a: a.reshape(1, -1)

    hx = x.reshape(batch * seq, d)
    hm = meta_tokens.astype(x.dtype)
    for l in range(depth):
        lam_init = 0.8 - 0.6 * math.exp(-0.3 * l)
        w_in_l = w_in[l].astype(BF16)
        lam_vecs = jnp.stack([lambda_q1[l], lambda_k1[l], lambda_q2[l], lambda_k2[l]]).astype(F32)
        g_sub = row(subln_g[l])
        zero_hist = jnp.zeros((SUBLANES, conv_w.shape[2]), F32)

        ypre_m, q_m, k_m, v_m, sga_m, sgb_m, u_tail = _in_proj(
            hm, row(norm_mix_pre[l]), w_in_l, conv_w[l], zero_hist, seq_rows=N_META)
        ypre, q, k, v, sga, sgb, _ = _in_proj(
            hx, row(norm_mix_pre[l]), w_in_l, conv_w[l], u_tail, seq_rows=seq)

        o_m = _meta_attention(q_m, k_m, v_m, slopes, lam_vecs, g_sub, lam_init=lam_init)
        o = _attention(q, k, v, k_m, v_m, slopes, lam_vecs, g_sub,
                       batch=batch, seq=seq, lam_init=lam_init)

        wc, wa, wm = (w.astype(BF16) for w in (w_conv_out[l], w_attn_out[l], w_mix_out[l]))
        hm = _merge(ypre_m, o_m, sga_m, sgb_m, hm, wc, wa, wm, row(norm_mix_post[l]))
        hx = _merge(ypre, o, sga, sgb, hx, wc, wa, wm, row(norm_mix_post[l]))

        wup, wdn = w_ffn_up[l].astype(BF16), w_ffn_down[l].astype(BF16)
        zero_hist = jnp.zeros((SUBLANES, wup.shape[1]), F32)
        ffn = functools.partial(_ffn, gpre=row(norm_ffn_pre[l]), wup=wup, conv_w=ffn_conv_w[l],
                                conv_b=row(ffn_conv_b[l]), wdn=wdn, gpost=row(norm_ffn_post[l]))
        hm, z_tail = ffn(hm, hist=zero_hist, seq_rows=N_META)
        hx, _ = ffn(hx, hist=z_tail, seq_rows=seq)
    return hx.reshape(batch, seq, d)
```

```python
import functools
import math

import numpy as np
import jax
import jax.numpy as jnp
from jax import lax
from jax.experimental import pallas as pl
from jax.experimental.pallas import tpu as pltpu

F32 = jnp.float32
BF16 = jnp.bfloat16

N_META = 16
N_HEADS = 8
HEAD_DIM = 64
V_DIM = 2 * HEAD_DIM
N_SEG = 8
RMS_EPS = 1e-6
NEG = -1e30

LANES = 128
SUBLANES = 8
ROW_TILE = 512
COL_CHUNK = 256
CONV_BUFFERS = 4
Q_TILE = 512
K_TILE = 512
ACC_ROWS = V_DIM + 16
VMEM_LIMIT = 56 * 1024 * 1024

LANE_POS_HI = HEAD_DIM
LANE_POS_LO = HEAD_DIM + 1
POS_SPLIT = 64


def _rms(x, g):
    ms = jnp.mean(x * x, axis=-1, keepdims=True)
    return x * lax.rsqrt(ms + RMS_EPS) * g


def _dot(a, b):
    return jnp.dot(a, b, preferred_element_type=F32)


def _conv_stage(z, buf_ref, carry_ref, cols):
    tm = z.shape[0]
    buf_ref[0:SUBLANES, :] = carry_ref[:, cols]
    buf_ref[SUBLANES:SUBLANES + tm, :] = z
    carry_ref[:, cols] = z[tm - SUBLANES:tm]


def _conv_apply(w, buf_ref):
    tm = buf_ref.shape[0] - SUBLANES
    return (w[2:3] * buf_ref[SUBLANES:SUBLANES + tm, :]
            + w[1:2] * buf_ref[SUBLANES - 1:SUBLANES - 1 + tm, :]
            + w[0:1] * buf_ref[SUBLANES - 2:SUBLANES - 2 + tm, :])


def _causal_conv3(z, w, buf_ref, carry_ref, cols):
    _conv_stage(z, buf_ref, carry_ref, cols)
    return _conv_apply(w, buf_ref)


def _in_proj_body(x_ref, g_ref, w_ref, cw_ref, hist_ref,
                  ypre_ref, q_ref, k_ref, v_ref, sga_ref, sgb_ref, tail_ref,
                  xn_ref, buf_ref, carry_ref, *, tiles_per_seq, width, q_scale):
    @pl.when(pl.program_id(0) % tiles_per_seq == 0)
    def _():
        carry_ref[...] = hist_ref[...]

    xn_ref[...] = _rms(x_ref[...], g_ref[...]).astype(BF16)

    for c in range(width // COL_CHUNK):
        cols = slice(c * COL_CHUNK, (c + 1) * COL_CHUNK)

        def proj(seg):
            lo = seg * width + c * COL_CHUNK
            return _dot(xn_ref[...], w_ref[:, lo:lo + COL_CHUNK])

        u = proj(1) * proj(2)
        y = _causal_conv3(u, cw_ref[:, cols], buf_ref.at[c % CONV_BUFFERS], carry_ref, cols)
        ypre_ref[:, cols] = (proj(0) * y).astype(BF16)
        q_ref[:, cols] = (proj(3) * q_scale).astype(BF16)
        k_ref[:, cols] = proj(4).astype(BF16)
        v_ref[:, cols] = proj(5).astype(BF16)
        sga_ref[:, cols] = jax.nn.sigmoid(proj(6)).astype(BF16)
        sgb_ref[:, cols] = jax.nn.sigmoid(proj(7)).astype(BF16)

    tail_ref[...] = carry_ref[...]


def _resident(shape):
    return pl.BlockSpec(shape, lambda i: (0,) * len(shape), pipeline_mode=pl.Buffered(1))


def _in_proj(h, gain, w_in, conv_w, hist, *, seq_rows):
    rows, d = h.shape
    width = w_in.shape[1] // N_SEG
    tm = min(ROW_TILE, rows)
    assert rows % tm == 0 and seq_rows % tm == 0 and width % COL_CHUNK == 0
    row_spec = lambda n: pl.BlockSpec((tm, n), lambda i: (i, 0))
    act = jax.ShapeDtypeStruct((rows, width), BF16)
    return pl.pallas_call(
        functools.partial(_in_proj_body, tiles_per_seq=seq_rows // tm, width=width,
                          q_scale=HEAD_DIM ** -0.5),
        grid=(rows // tm,),
        in_specs=[row_spec(d), _resident((1, d)), _resident(w_in.shape), _resident(conv_w.shape),
                  _resident(hist.shape)],
        out_specs=[row_spec(width)] * 6 + [pl.BlockSpec((SUBLANES, width), lambda i: (0, 0))],
        out_shape=[act] * 6 + [jax.ShapeDtypeStruct((SUBLANES, width), F32)],
        scratch_shapes=[pltpu.VMEM((tm, d), BF16),
                        pltpu.VMEM((CONV_BUFFERS, tm + SUBLANES, COL_CHUNK), F32),
                        pltpu.VMEM((SUBLANES, width), F32)],
        compiler_params=pltpu.CompilerParams(dimension_semantics=("arbitrary",),
                                             vmem_limit_bytes=VMEM_LIMIT),
        name="in_proj",
    )(h, gain, w_in, conv_w, hist)


def _lam(lam_ref, lam_init):
    l = lam_ref[...]
    return (jnp.exp(jnp.sum(l[0:1] * l[1:2], axis=-1, keepdims=True))
            - jnp.exp(jnp.sum(l[2:3] * l[3:4], axis=-1, keepdims=True)) + lam_init)


def _subln(o, g, lam_init):
    return _rms(o, g) * (1.0 - lam_init)


def _key_features(pos0, n):
    pos = lax.broadcasted_iota(jnp.int32, (n, LANES), 0) + pos0
    lane = lax.broadcasted_iota(jnp.int32, (n, LANES), 1)
    hi = (pos >> 6).astype(F32)
    lo = (pos & (POS_SPLIT - 1)).astype(F32)
    feat = jnp.where(lane == LANE_POS_HI, hi, jnp.where(lane == LANE_POS_LO, lo, 0.0))
    return feat, lane


def _split_maps(x, feat, lane):
    first = jnp.where(lane < HEAD_DIM, x, feat)
    second = jnp.where(lane < HEAD_DIM, pltpu.roll(x, HEAD_DIM, axis=1), feat)
    return first, second


def _ones_rows(n):
    return jnp.where(lax.broadcasted_iota(jnp.int32, (ACC_ROWS - V_DIM, n), 0) == 0, 1.0, 0.0)


def _attn_body(q_ref, k_ref, v_ref, mk_ref, mv_ref, slope_ref, lam_ref, g_ref, o_ref,
               qt1_ref, qt2_ref, ka1_ref, ka2_ref, vt_ref, mask_ref,
               m01_ref, m02_ref, acc01_ref, acc02_ref,
               s1a_ref, s1b_ref, s2a_ref, s2b_ref, tm1a_ref, tm1b_ref, tm2a_ref, tm2b_ref,
               acc1_ref, acc2_ref, *, lam_init):
    n_q, _, tq = qt1_ref.shape
    n_k, tk, _ = ka1_ref.shape
    s1_ref, s2_ref = (s1a_ref, s1b_ref), (s2a_ref, s2b_ref)
    tmax1_ref, tmax2_ref = (tm1a_ref, tm1b_ref), (tm2a_ref, tm2b_ref)
    ratio = tq // tk
    n_items = ratio * n_q * (n_q + 1) // 2

    @pl.when((pl.program_id(0) == 0) & (pl.program_id(1) == 0))
    def _build_masks():
        kk = lax.broadcasted_iota(jnp.int32, (tk, tq), 0)
        qq = lax.broadcasted_iota(jnp.int32, (tk, tq), 1)
        for d in range(ratio):
            mask_ref[d] = jnp.where(kk + d * tk <= qq, 0.0, NEG)

    def key_tile(t, carry):
        r0 = pl.multiple_of(t * tk, tk)
        feat, lane = _key_features(N_META + r0, tk)
        k1, k2 = _split_maps(k_ref[pl.ds(r0, tk), :].astype(F32), feat, lane)
        ka1_ref[t] = k1.astype(BF16)
        ka2_ref[t] = k2.astype(BF16)
        vt_ref[t, 0:V_DIM, :] = v_ref[pl.ds(r0, tk), :].astype(F32).T.astype(BF16)
        vt_ref[t, V_DIM:ACC_ROWS, :] = _ones_rows(tk).astype(BF16)
        return carry

    lax.fori_loop(0, n_k, key_tile, 0)

    feat, lane = _key_features(0, N_META)
    mk1, mk2 = (a.astype(BF16) for a in _split_maps(mk_ref[...].astype(F32), feat, lane))
    mv_pad = jnp.concatenate([mv_ref[...].astype(F32), jnp.zeros((LANES - N_META, LANES), F32)], axis=0)
    mvt = jnp.concatenate([mv_pad.T[:, 0:N_META], _ones_rows(N_META)], axis=0).astype(BF16)

    slope = slope_ref[...]

    def query_tile(i, carry):
        r0 = i * tq
        lane = lax.broadcasted_iota(jnp.int32, (tq, LANES), 1)
        qfeat = jnp.where(lane == LANE_POS_HI, slope * POS_SPLIT,
                          jnp.where(lane == LANE_POS_LO, slope, 0.0))
        q1, q2 = _split_maps(q_ref[pl.ds(r0, tq), :].astype(F32), qfeat, lane)
        for qm, mk, qt_ref, m0_ref, acc0_ref in ((q1, mk1, qt1_ref, m01_ref, acc01_ref),
                                                 (q2, mk2, qt2_ref, m02_ref, acc02_ref)):
            qt = qm.T.astype(BF16)
            qt_ref[i] = qt
            s_meta = _dot(mk, qt)
            m0 = jnp.max(s_meta, axis=0, keepdims=True)
            m0_ref[i] = m0
            acc0_ref[i] = _dot(mvt, jnp.exp(s_meta - m0).astype(BF16))
        return carry

    for i in range(n_q):
        query_tile(i, 0)

    acc1_ref[...] = acc01_ref[0]
    acc2_ref[...] = acc02_ref[0]

    def stage_a(qi, t, slot, masked):
        for ka_ref, qt_ref, s_ref, tmax_ref in ((ka1_ref, qt1_ref, s1_ref, tmax1_ref),
                                                (ka2_ref, qt2_ref, s2_ref, tmax2_ref)):
            s = _dot(ka_ref[t], qt_ref[qi])
            if masked:
                s = s + mask_ref[jnp.maximum(t - qi * ratio, 0)]
            s_ref[slot][...] = s
            tmax_ref[slot][...] = jnp.max(s, axis=0, keepdims=True)

    def stage_b(qi, t, slot, ms):
        first = t == 0
        out = []
        for s_ref, tmax_ref, acc_ref, m0_ref, m in ((s1_ref, tmax1_ref, acc1_ref, m01_ref, ms[0]),
                                                    (s2_ref, tmax2_ref, acc2_ref, m02_ref, ms[1])):
            m_old = jnp.where(first, m0_ref[qi], m)
            m_new = jnp.maximum(m_old, tmax_ref[slot][...])
            p = jnp.exp(s_ref[slot][...] - m_new).astype(BF16)
            acc_ref[...] = jnp.exp(m_old - m_new) * acc_ref[...] + _dot(vt_ref[t], p)
            out.append(m_new)
        return tuple(out)

    def advance(qi, t):
        end = t + 1 == (qi + 1) * ratio
        more = qi + 1 < n_q
        return (jnp.where(end & more, qi + 1, qi),
                jnp.where(end, jnp.where(more, 0, t), t + 1))

    def iteration(w, slot_b, carry):
        qa, ta, qb, tb, m1, m2 = carry

        def run(masked):
            def f():
                stage_a(qa, ta, 1 - slot_b, masked)
                return stage_b(qb, tb, slot_b, (m1, m2))
            return f

        m1, m2 = lax.cond(ta >= qa * ratio, run(True), run(False))

        @pl.when((w < n_items) & (tb + 1 == (qb + 1) * ratio))
        def _finish_query_tile():
            w1 = 1.0 / acc1_ref[V_DIM:V_DIM + 1, :]
            w2 = _lam(lam_ref, lam_init) / acc2_ref[V_DIM:V_DIM + 1, :]
            o_t = acc1_ref[0:V_DIM, :] * w1 - acc2_ref[0:V_DIM, :] * w2
            scale = lax.rsqrt(jnp.mean(o_t * o_t, axis=0, keepdims=True) + RMS_EPS) * (1.0 - lam_init)
            r0 = pl.multiple_of(qb * tq, tq)
            o_ref[pl.ds(r0, tq), :] = ((o_t * scale).T * g_ref[...]).astype(BF16)
            nxt = jnp.minimum(qb + 1, n_q - 1)
            acc1_ref[...] = acc01_ref[nxt]
            acc2_ref[...] = acc02_ref[nxt]

        qn, tn = advance(qa, ta)
        return qn, tn, qa, ta, m1, m2

    zero = jnp.int32(0)
    stage_a(zero, zero, 0, True)
    q1_, t1_ = advance(zero, zero)
    m_init = jnp.full((1, tq), NEG, F32)

    def iteration_pair(i, carry):
        return iteration(2 * i + 1, 1, iteration(2 * i, 0, carry))

    lax.fori_loop(0, (n_items + 1) // 2, iteration_pair, (q1_, t1_, zero, zero, m_init, m_init))


def _attention(q, k, v, mk, mv, slopes, lam_vecs, g, *, batch, seq, lam_init):
    rows, width = q.shape
    tq, tk = min(Q_TILE, seq), min(K_TILE, seq)
    n_q, n_k = seq // tq, seq // tk
    assert seq % tq == 0 and tq % tk == 0 and tk % LANES == 0 and width == N_HEADS * V_DIM
    full = lambda a: pl.BlockSpec(a.shape, lambda b, h: (0,) * a.ndim)
    seq_spec = pl.BlockSpec((seq, V_DIM), lambda b, h: (b, h))
    meta_spec = pl.BlockSpec((N_META, V_DIM), lambda b, h: (0, h))
    vmem = pltpu.VMEM
    return pl.pallas_call(
        functools.partial(_attn_body, lam_init=lam_init),
        grid=(batch, N_HEADS),
        in_specs=[seq_spec, seq_spec, seq_spec, meta_spec, meta_spec,
                  pl.BlockSpec((None, 1, LANES), lambda b, h: (h, 0, 0)), full(lam_vecs), full(g)],
        out_specs=seq_spec,
        out_shape=jax.ShapeDtypeStruct((rows, width), BF16),
        scratch_shapes=[vmem((n_q, LANES, tq), BF16), vmem((n_q, LANES, tq), BF16),
                        vmem((n_k, tk, LANES), BF16), vmem((n_k, tk, LANES), BF16),
                        vmem((n_k, ACC_ROWS, tk), BF16),
                        vmem((tq // tk, tk, tq), F32),
                        vmem((n_q, 1, tq), F32), vmem((n_q, 1, tq), F32),
                        vmem((n_q, ACC_ROWS, tq), F32), vmem((n_q, ACC_ROWS, tq), F32),
                        vmem((tk, tq), F32), vmem((tk, tq), F32), vmem((tk, tq), F32), vmem((tk, tq), F32),
                        vmem((1, tq), F32), vmem((1, tq), F32), vmem((1, tq), F32), vmem((1, tq), F32),
                        vmem((ACC_ROWS, tq), F32), vmem((ACC_ROWS, tq), F32)],
        compiler_params=pltpu.CompilerParams(dimension_semantics=("arbitrary", "arbitrary"),
                                             vmem_limit_bytes=VMEM_LIMIT),
        name="diff_attention",
    )(q, k, v, mk, mv, slopes, lam_vecs, g)


def _meta_attn_body(q_ref, k_ref, v_ref, slope_ref, lam_ref, g_ref, o_ref, *, lam_init):
    q = q_ref[...]
    k = k_ref[...]
    n = q.shape[0]
    lane = lax.broadcasted_iota(jnp.int32, q.shape, 1)
    zero = jnp.zeros_like(q)
    contract_last = (((1,), (1,)), ((), ()))
    s1 = lax.dot_general(jnp.where(lane < HEAD_DIM, q, zero), k, contract_last,
                         preferred_element_type=F32)
    s2 = lax.dot_general(jnp.where(lane >= HEAD_DIM, q, zero), k, contract_last,
                         preferred_element_type=F32)
    qpos = lax.broadcasted_iota(jnp.int32, (n, n), 0)
    kpos = lax.broadcasted_iota(jnp.int32, (n, n), 1)
    bias = -slope_ref[...][:, 0:1] * (qpos - kpos).astype(F32)

    def softmax(s):
        s = jnp.where(kpos <= qpos, s + bias, NEG)
        e = jnp.exp(s - jnp.max(s, axis=-1, keepdims=True))
        return e / jnp.sum(e, axis=-1, keepdims=True)

    w = softmax(s1) - _lam(lam_ref, lam_init) * softmax(s2)
    o_ref[...] = _subln(_dot(w.astype(BF16), v_ref[...]), g_ref[...], lam_init).astype(BF16)


def _meta_attention(q, k, v, slopes, lam_vecs, g, *, lam_init):
    full = lambda a: pl.BlockSpec(a.shape, lambda h: (0,) * a.ndim)
    head = pl.BlockSpec((N_META, V_DIM), lambda h: (0, h))
    return pl.pallas_call(
        functools.partial(_meta_attn_body, lam_init=lam_init),
        grid=(N_HEADS,),
        in_specs=[head, head, head, pl.BlockSpec((None, 1, LANES), lambda h: (h, 0, 0)),
                  full(lam_vecs), full(g)],
        out_specs=head,
        out_shape=jax.ShapeDtypeStruct(q.shape, BF16),
        name="meta_attention",
    )(q, k, v, slopes, lam_vecs, g)


def _merge_body(ypre_ref, o_ref, sga_ref, sgb_ref, h_ref, wc_ref, wa_ref, wm_ref, g_ref, out_ref):
    ya = _dot(ypre_ref[...], wc_ref[...])
    yb = _dot(o_ref[...], wa_ref[...])
    gated = sga_ref[...].astype(F32) * ya + sgb_ref[...].astype(F32) * yb
    mix = _dot(gated.astype(BF16), wm_ref[...])
    out_ref[...] = h_ref[...] + _rms(mix, g_ref[...])


def _merge(ypre, o, sga, sgb, h, wc, wa, wm, g):
    rows, d = h.shape
    tm = min(ROW_TILE, rows)
    assert rows % tm == 0
    row_spec = lambda n: pl.BlockSpec((tm, n), lambda i: (i, 0))
    return pl.pallas_call(
        _merge_body,
        grid=(rows // tm,),
        in_specs=[row_spec(ypre.shape[1]), row_spec(o.shape[1]), row_spec(sga.shape[1]),
                  row_spec(sgb.shape[1]), row_spec(d),
                  _resident(wc.shape), _resident(wa.shape), _resident(wm.shape), _resident(g.shape)],
        out_specs=row_spec(d),
        out_shape=jax.ShapeDtypeStruct((rows, d), F32),
        compiler_params=pltpu.CompilerParams(dimension_semantics=("arbitrary",),
                                             vmem_limit_bytes=VMEM_LIMIT),
        name="merge",
    )(ypre, o, sga, sgb, h, wc, wa, wm, g)


def _ffn_body(h_ref, gpre_ref, wup_ref, cw_ref, cb_ref, hist_ref, wdn_ref, gpost_ref,
              out_ref, tail_ref, f_ref, buf_ref, carry_ref, acc_ref, *, tiles_per_seq, d_ff):
    @pl.when(pl.program_id(0) % tiles_per_seq == 0)
    def _():
        carry_ref[...] = hist_ref[...]

    f_ref[...] = _rms(h_ref[...], gpre_ref[...]).astype(BF16)

    n_chunks = d_ff // COL_CHUNK
    half_cols = lambda c, half: slice(half * d_ff + c * COL_CHUNK, half * d_ff + (c + 1) * COL_CHUNK)
    half_buf = lambda c, half: buf_ref.at[(2 * c + half) % CONV_BUFFERS]

    def up(c):
        for half in range(2):
            cols = half_cols(c, half)
            _conv_stage(_dot(f_ref[...], wup_ref[:, cols]), half_buf(c, half), carry_ref, cols)

    def gate_down(c):
        g, u = (_conv_apply(cw_ref[:, half_cols(c, half)], half_buf(c, half)) + cb_ref[:, half_cols(c, half)]
                for half in range(2))
        a = (jax.nn.silu(g) * u).astype(BF16)
        return _dot(a, wdn_ref[c * COL_CHUNK:(c + 1) * COL_CHUNK, :])

    up(0)
    for c in range(n_chunks):
        if c + 1 < n_chunks:
            up(c + 1)
        y = gate_down(c)
        if c == 0:
            acc_ref[...] = y
        else:
            acc_ref[...] += y

    tail_ref[...] = carry_ref[...]
    out_ref[...] = h_ref[...] + _rms(acc_ref[...], gpost_ref[...])


def _ffn(h, gpre, wup, conv_w, conv_b, hist, wdn, gpost, *, seq_rows):
    rows, d = h.shape
    d_ff = wdn.shape[0]
    tm = min(ROW_TILE, rows)
    assert rows % tm == 0 and seq_rows % tm == 0 and d_ff % COL_CHUNK == 0
    row_spec = pl.BlockSpec((tm, d), lambda i: (i, 0))
    return pl.pallas_call(
        functools.partial(_ffn_body, tiles_per_seq=seq_rows // tm, d_ff=d_ff),
        grid=(rows // tm,),
        in_specs=[row_spec, _resident(gpre.shape), _resident(wup.shape), _resident(conv_w.shape),
                  _resident(conv_b.shape), _resident(hist.shape), _resident(wdn.shape),
                  _resident(gpost.shape)],
        out_specs=[row_spec, pl.BlockSpec((SUBLANES, 2 * d_ff), lambda i: (0, 0))],
        out_shape=[jax.ShapeDtypeStruct((rows, d), F32),
                   jax.ShapeDtypeStruct((SUBLANES, 2 * d_ff), F32)],
        scratch_shapes=[pltpu.VMEM((tm, d), BF16),
                        pltpu.VMEM((CONV_BUFFERS, tm + SUBLANES, COL_CHUNK), F32),
                        pltpu.VMEM((SUBLANES, 2 * d_ff), F32),
                        pltpu.VMEM((tm, d), F32)],
        compiler_params=pltpu.CompilerParams(dimension_semantics=("arbitrary",),
                                             vmem_limit_bytes=VMEM_LIMIT),
        name="ffn",
    )(h, gpre, wup, conv_w, conv_b, hist, wdn, gpost)


def _alibi_slopes():
    slopes = 2.0 ** (-8.0 * np.arange(1, N_HEADS + 1, dtype=np.float64) / N_HEADS)
    assert np.array_equal(slopes.astype(BF16).astype(np.float64), slopes), "slopes must be exact in bf16"
    return jnp.asarray(np.broadcast_to(slopes[:, None, None], (N_HEADS, 1, LANES)), F32)


def kernel(x, meta_tokens, w_in, conv_w, w_conv_out, lambda_q1, lambda_k1, lambda_q2, lambda_k2,
           subln_g, w_attn_out, w_mix_out, norm_mix_pre, norm_mix_post, w_ffn_up, ffn_conv_w,
           ffn_conv_b, w_ffn_down, norm_ffn_pre, norm_ffn_post):
    batch, seq, d = x.shape
    depth = w_in.shape[0]
    assert meta_tokens.shape[0] == N_META and seq + N_META <= POS_SPLIT * 256
    slopes = _alibi_slopes()
    row = lambda a: a.reshape(1, -1)

    hx = x.reshape(batch * seq, d)
    hm = meta_tokens.astype(x.dtype)
    for l in range(depth):
        lam_init = 0.8 - 0.6 * math.exp(-0.3 * l)
        w_in_l = w_in[l].astype(BF16)
        lam_vecs = jnp.stack([lambda_q1[l], lambda_k1[l], lambda_q2[l], lambda_k2[l]]).astype(F32)
        g_sub = row(subln_g[l])
        zero_hist = jnp.zeros((SUBLANES, conv_w.shape[2]), F32)

        ypre_m, q_m, k_m, v_m, sga_m, sgb_m, u_tail = _in_proj(
            hm, row(norm_mix_pre[l]), w_in_l, conv_w[l], zero_hist, seq_rows=N_META)
        ypre, q, k, v, sga, sgb, _ = _in_proj(
            hx, row(norm_mix_pre[l]), w_in_l, conv_w[l], u_tail, seq_rows=seq)

        o_m = _meta_attention(q_m, k_m, v_m, slopes, lam_vecs, g_sub, lam_init=lam_init)
        o = _attention(q, k, v, k_m, v_m, slopes, lam_vecs, g_sub,
                       batch=batch, seq=seq, lam_init=lam_init)

        wc, wa, wm = (w.astype(BF16) for w in (w_conv_out[l], w_attn_out[l], w_mix_out[l]))
        hm = _merge(ypre_m, o_m, sga_m, sgb_m, hm, wc, wa, wm, row(norm_mix_post[l]))
        hx = _merge(ypre, o, sga, sgb, hx, wc, wa, wm, row(norm_mix_post[l]))

        wup, wdn = w_ffn_up[l].astype(BF16), w_ffn_down[l].astype(BF16)
        zero_hist = jnp.zeros((SUBLANES, wup.shape[1]), F32)
        ffn = functools.partial(_ffn, gpre=row(norm_ffn_pre[l]), wup=wup, conv_w=ffn_conv_w[l],
                                conv_b=row(ffn_conv_b[l]), wdn=wdn, gpost=row(norm_ffn_post[l]))
        hm, z_tail = ffn(hm, hist=zero_hist, seq_rows=N_META)
        hx, _ = ffn(hx, hist=z_tail, seq_rows=seq)
    return hx.reshape(batch, seq, d)
```

```python
import functools
import math

import numpy as np
import jax
import jax.numpy as jnp
from jax import lax
from jax.experimental import pallas as pl
from jax.experimental.pallas import tpu as pltpu

F32 = jnp.float32
BF16 = jnp.bfloat16

N_META = 16
N_HEADS = 8
HEAD_DIM = 64
V_DIM = 2 * HEAD_DIM
N_SEG = 8
RMS_EPS = 1e-6
NEG = -1e30

LANES = 128
SUBLANES = 8
ROW_TILE = 512
COL_CHUNK = 256
CONV_BUFFERS = 4
Q_TILE = ROW_TILE
K_TILE = ROW_TILE
ACC_ROWS = V_DIM + 16
VMEM_LIMIT = 56 * 1024 * 1024

LANE_POS_HI = HEAD_DIM
LANE_POS_LO = HEAD_DIM + 1
POS_SPLIT = 64


def _rms(x, g):
    ms = jnp.mean(x * x, axis=-1, keepdims=True)
    return x * lax.rsqrt(ms + RMS_EPS) * g


def _dot(a, b):
    return jnp.dot(a, b, preferred_element_type=F32)


def _conv_stage(z, buf_ref, carry_ref, cols):
    tm = z.shape[0]
    buf_ref[0:SUBLANES, :] = carry_ref[:, cols]
    buf_ref[SUBLANES:SUBLANES + tm, :] = z
    carry_ref[:, cols] = z[tm - SUBLANES:tm]


def _conv_apply(w, buf_ref):
    tm = buf_ref.shape[0] - SUBLANES
    return (w[2:3] * buf_ref[SUBLANES:SUBLANES + tm, :]
            + w[1:2] * buf_ref[SUBLANES - 1:SUBLANES - 1 + tm, :]
            + w[0:1] * buf_ref[SUBLANES - 2:SUBLANES - 2 + tm, :])


def _causal_conv3(z, w, buf_ref, carry_ref, cols):
    _conv_stage(z, buf_ref, carry_ref, cols)
    return _conv_apply(w, buf_ref)


def _alibi_slopes():
    slopes = 2.0 ** (-8.0 * np.arange(1, N_HEADS + 1, dtype=np.float64) / N_HEADS)
    assert np.array_equal(slopes.astype(BF16).astype(np.float64), slopes), "slopes must be exact in bf16"
    return slopes


def _key_features(pos0, n):
    pos = lax.broadcasted_iota(jnp.int32, (n, LANES), 0) + pos0
    lane = lax.broadcasted_iota(jnp.int32, (n, LANES), 1)
    hi = (pos >> 6).astype(F32)
    lo = (pos & (POS_SPLIT - 1)).astype(F32)
    feat = jnp.where(lane == LANE_POS_HI, hi, jnp.where(lane == LANE_POS_LO, lo, 0.0))
    return feat, lane


def _query_features(slope, lane):
    return jnp.where(lane == LANE_POS_HI, slope * POS_SPLIT, jnp.where(lane == LANE_POS_LO, slope, 0.0))


def _split_maps(x, feat, lane):
    first = jnp.where(lane < HEAD_DIM, x, feat)
    second = jnp.where(lane < HEAD_DIM, pltpu.roll(x, HEAD_DIM, axis=1), feat)
    return first, second


def _ones_rows(n):
    return jnp.where(lax.broadcasted_iota(jnp.int32, (ACC_ROWS - V_DIM, n), 0) == 0, 1.0, 0.0)


def _in_proj_body(x_ref, g_ref, w_ref, cw_ref, hist_ref,
                  ypre_ref, q_ref, k_ref, v_ref, sga_ref, sgb_ref, tail_ref,
                  xn_ref, buf_ref, carry_ref, *, tiles_per_seq, width, q_scale, attn_layout):
    tile_in_seq = pl.program_id(0) % tiles_per_seq

    @pl.when(tile_in_seq == 0)
    def _():
        carry_ref[...] = hist_ref[...]

    xn_ref[...] = _rms(x_ref[...], g_ref[...]).astype(BF16)
    tm = x_ref.shape[0]
    heads_per_chunk = COL_CHUNK // V_DIM
    if attn_layout:
        slopes = _alibi_slopes()
        kfeat, lane = _key_features(N_META + tile_in_seq * tm, tm)

    for c in range(width // COL_CHUNK):
        cols = slice(c * COL_CHUNK, (c + 1) * COL_CHUNK)

        def proj(seg):
            lo = seg * width + c * COL_CHUNK
            return _dot(xn_ref[...], w_ref[:, lo:lo + COL_CHUNK])

        u = proj(1) * proj(2)
        y = _causal_conv3(u, cw_ref[:, cols], buf_ref.at[c % CONV_BUFFERS], carry_ref, cols)
        ypre_ref[:, cols] = (proj(0) * y).astype(BF16)
        sga_ref[:, cols] = jax.nn.sigmoid(proj(6)).astype(BF16)
        sgb_ref[:, cols] = jax.nn.sigmoid(proj(7)).astype(BF16)
        q, k, v = proj(3) * q_scale, proj(4), proj(5)
        if not attn_layout:
            q_ref[:, cols] = q.astype(BF16)
            k_ref[:, cols] = k.astype(BF16)
            v_ref[:, cols] = v.astype(BF16)
            continue
        for hh in range(heads_per_chunk):
            head = c * heads_per_chunk + hh
            lanes = slice(hh * V_DIM, (hh + 1) * V_DIM)
            qfeat = _query_features(float(slopes[head]), lane)
            for m, qm in enumerate(_split_maps(q[:, lanes], qfeat, lane)):
                q_ref[head, m, 0] = qm.T.astype(BF16)
            for m, km in enumerate(_split_maps(k[:, lanes], kfeat, lane)):
                k_ref[:, (2 * head + m) * LANES:(2 * head + m + 1) * LANES] = km.astype(BF16)
            v_ref[head, 0, 0:V_DIM, :] = v[:, lanes].T.astype(BF16)
            v_ref[head, 0, V_DIM:ACC_ROWS, :] = _ones_rows(tm).astype(BF16)

    tail_ref[...] = carry_ref[...]


def _resident(shape):
    return pl.BlockSpec(shape, lambda i: (0,) * len(shape), pipeline_mode=pl.Buffered(1))


def _in_proj(h, gain, w_in, conv_w, hist, *, seq_rows, attn_layout):
    rows, d = h.shape
    width = w_in.shape[1] // N_SEG
    tm = min(ROW_TILE, rows)
    n_tiles = rows // tm
    assert rows % tm == 0 and seq_rows % tm == 0 and width % COL_CHUNK == 0
    assert width == N_HEADS * V_DIM and COL_CHUNK % V_DIM == 0
    row_spec = lambda n: pl.BlockSpec((tm, n), lambda i: (i, 0))
    act = jax.ShapeDtypeStruct((rows, width), BF16)
    if attn_layout:
        qkv_specs = [pl.BlockSpec((N_HEADS, 2, 1, LANES, tm), lambda i: (0, 0, i, 0, 0)),
                     row_spec(2 * width),
                     pl.BlockSpec((N_HEADS, 1, ACC_ROWS, tm), lambda i: (0, i, 0, 0))]
        qkv_shapes = [jax.ShapeDtypeStruct((N_HEADS, 2, n_tiles, LANES, tm), BF16),
                      jax.ShapeDtypeStruct((rows, 2 * width), BF16),
                      jax.ShapeDtypeStruct((N_HEADS, n_tiles, ACC_ROWS, tm), BF16)]
    else:
        qkv_specs, qkv_shapes = [row_spec(width)] * 3, [act] * 3
    return pl.pallas_call(
        functools.partial(_in_proj_body, tiles_per_seq=seq_rows // tm, width=width,
                          q_scale=HEAD_DIM ** -0.5, attn_layout=attn_layout),
        grid=(n_tiles,),
        in_specs=[row_spec(d), _resident((1, d)), _resident(w_in.shape), _resident(conv_w.shape),
                  _resident(hist.shape)],
        out_specs=[row_spec(width)] + qkv_specs + [row_spec(width)] * 2
                  + [pl.BlockSpec((SUBLANES, width), lambda i: (0, 0))],
        out_shape=[act] + qkv_shapes + [act] * 2 + [jax.ShapeDtypeStruct((SUBLANES, width), F32)],
        scratch_shapes=[pltpu.VMEM((tm, d), BF16),
                        pltpu.VMEM((CONV_BUFFERS, tm + SUBLANES, COL_CHUNK), F32),
                        pltpu.VMEM((SUBLANES, width), F32)],
        compiler_params=pltpu.CompilerParams(dimension_semantics=("arbitrary",),
                                             vmem_limit_bytes=VMEM_LIMIT),
        name="in_proj",
    )(h, gain, w_in, conv_w, hist)


def _lam(lam_ref, lam_init):
    l = lam_ref[...]
    return (jnp.exp(jnp.sum(l[0:1] * l[1:2], axis=-1, keepdims=True))
            - jnp.exp(jnp.sum(l[2:3] * l[3:4], axis=-1, keepdims=True)) + lam_init)


def _subln(o, g, lam_init):
    return _rms(o, g) * (1.0 - lam_init)


def _attn_body(qt_ref, ka1_ref, ka2_ref, vt_ref, mk_ref, mv_ref, lam_ref, g_ref, o_ref,
               mask_ref, m01_ref, m02_ref, acc01_ref, acc02_ref,
               s1a_ref, s1b_ref, s2a_ref, s2b_ref, tm1a_ref, tm1b_ref, tm2a_ref, tm2b_ref,
               acc1_ref, acc2_ref, *, lam_init):
    _, n_q, _, tq = qt_ref.shape
    n_k, _, tk = vt_ref.shape
    s1_ref, s2_ref = (s1a_ref, s1b_ref), (s2a_ref, s2b_ref)
    tmax1_ref, tmax2_ref = (tm1a_ref, tm1b_ref), (tm2a_ref, tm2b_ref)
    ratio = tq // tk
    n_items = ratio * n_q * (n_q + 1) // 2

    @pl.when((pl.program_id(0) == 0) & (pl.program_id(1) == 0))
    def _build_masks():
        kk = lax.broadcasted_iota(jnp.int32, (tk, tq), 0)
        qq = lax.broadcasted_iota(jnp.int32, (tk, tq), 1)
        for d in range(ratio):
            mask_ref[d] = jnp.where(kk + d * tk <= qq, 0.0, NEG)

    feat, lane = _key_features(0, N_META)
    mk1, mk2 = (a.astype(BF16) for a in _split_maps(mk_ref[...].astype(F32), feat, lane))
    mv_pad = jnp.concatenate([mv_ref[...].astype(F32), jnp.zeros((LANES - N_META, LANES), F32)], axis=0)
    mvt = jnp.concatenate([mv_pad.T[:, 0:N_META], _ones_rows(N_META)], axis=0).astype(BF16)

    pairs = [(i, m) for i in range(n_q) for m in range(2)]
    m0_refs, acc0_refs = (m01_ref, m02_ref), (acc01_ref, acc02_ref)
    s_meta = [_dot((mk1, mk2)[m], qt_ref[m, i]) for i, m in pairs]
    m0 = [jnp.max(s, axis=0, keepdims=True) for s in s_meta]
    p_meta = [jnp.exp(s - mx).astype(BF16) for s, mx in zip(s_meta, m0)]
    for (i, m), mx, p in zip(pairs, m0, p_meta):
        m0_refs[m][i] = mx
        acc0_refs[m][i] = _dot(mvt, p)

    acc1_ref[...] = acc01_ref[0]
    acc2_ref[...] = acc02_ref[0]

    def stage_a(qi, t, slot, masked):
        r0 = pl.multiple_of(t * tk, tk)
        for m, (ka_ref, s_ref, tmax_ref) in enumerate(((ka1_ref, s1_ref, tmax1_ref),
                                                       (ka2_ref, s2_ref, tmax2_ref))):
            s = _dot(ka_ref[pl.ds(r0, tk), :], qt_ref[m, qi])
            if masked:
                s = s + mask_ref[jnp.maximum(t - qi * ratio, 0)]
            s_ref[slot][...] = s
            tmax_ref[slot][...] = jnp.max(s, axis=0, keepdims=True)

    def stage_b(qi, t, slot, ms):
        first = t == 0
        out = []
        for s_ref, tmax_ref, acc_ref, m0_ref, m in ((s1_ref, tmax1_ref, acc1_ref, m01_ref, ms[0]),
                                                    (s2_ref, tmax2_ref, acc2_ref, m02_ref, ms[1])):
            m_old = jnp.where(first, m0_ref[qi], m)
            m_new = jnp.maximum(m_old, tmax_ref[slot][...])
            p = jnp.exp(s_ref[slot][...] - m_new).astype(BF16)
            acc_ref[...] = jnp.exp(m_old - m_new) * acc_ref[...] + _dot(vt_ref[t], p)
            out.append(m_new)
        return tuple(out)

    def advance(qi, t):
        end = t + 1 == (qi + 1) * ratio
        more = qi + 1 < n_q
        return (jnp.where(end & more, qi + 1, qi),
                jnp.where(end, jnp.where(more, 0, t), t + 1))

    def iteration(w, slot_b, carry):
        qa, ta, qb, tb, m1, m2 = carry

        def run(masked):
            def f():
                stage_a(qa, ta, 1 - slot_b, masked)
                return stage_b(qb, tb, slot_b, (m1, m2))
            return f

        m1, m2 = lax.cond(ta >= qa * ratio, run(True), run(False))

        @pl.when((w < n_items) & (tb + 1 == (qb + 1) * ratio))
        def _finish_query_tile():
            w1 = 1.0 / acc1_ref[V_DIM:V_DIM + 1, :]
            w2 = _lam(lam_ref, lam_init) / acc2_ref[V_DIM:V_DIM + 1, :]
            o_t = acc1_ref[0:V_DIM, :] * w1 - acc2_ref[0:V_DIM, :] * w2
            scale = lax.rsqrt(jnp.mean(o_t * o_t, axis=0, keepdims=True) + RMS_EPS) * (1.0 - lam_init)
            r0 = pl.multiple_of(qb * tq, tq)
            o_ref[pl.ds(r0, tq), :] = ((o_t * scale).T * g_ref[...]).astype(BF16)
            nxt = jnp.minimum(qb + 1, n_q - 1)
            acc1_ref[...] = acc01_ref[nxt]
            acc2_ref[...] = acc02_ref[nxt]

        qn, tn = advance(qa, ta)
        return qn, tn, qa, ta, m1, m2

    zero = jnp.int32(0)
    stage_a(zero, zero, 0, True)
    q1_, t1_ = advance(zero, zero)
    m_init = jnp.full((1, tq), NEG, F32)

    def iteration_pair(i, carry):
        return iteration(2 * i + 1, 1, iteration(2 * i, 0, carry))

    lax.fori_loop(0, (n_items + 1) // 2, iteration_pair, (q1_, t1_, zero, zero, m_init, m_init))


def _attention(qt, ka, vt, mk, mv, lam_vecs, g, *, batch, seq, lam_init):
    tq, tk = qt.shape[-1], vt.shape[-1]
    n_q, n_k = seq // tq, seq // tk
    assert seq % tq == 0 and tq % tk == 0 and tk % LANES == 0
    assert qt.shape[2] == batch * n_q and vt.shape[1] == batch * n_k
    full = lambda a: pl.BlockSpec(a.shape, lambda b, h: (0,) * a.ndim)
    meta_spec = pl.BlockSpec((N_META, V_DIM), lambda b, h: (0, h))
    key_spec = lambda m: pl.BlockSpec((seq, LANES), lambda b, h: (b, 2 * h + m))
    vmem = pltpu.VMEM
    return pl.pallas_call(
        functools.partial(_attn_body, lam_init=lam_init),
        grid=(batch, N_HEADS),
        in_specs=[pl.BlockSpec((None, 2, n_q, LANES, tq), lambda b, h: (h, 0, b, 0, 0)),
                  key_spec(0), key_spec(1),
                  pl.BlockSpec((None, n_k, ACC_ROWS, tk), lambda b, h: (h, b, 0, 0)),
                  meta_spec, meta_spec, full(lam_vecs), full(g)],
        out_specs=pl.BlockSpec((seq, V_DIM), lambda b, h: (b, h)),
        out_shape=jax.ShapeDtypeStruct((batch * seq, N_HEADS * V_DIM), BF16),
        scratch_shapes=[vmem((tq // tk, tk, tq), F32),
                        vmem((n_q, 1, tq), F32), vmem((n_q, 1, tq), F32),
                        vmem((n_q, ACC_ROWS, tq), F32), vmem((n_q, ACC_ROWS, tq), F32),
                        vmem((tk, tq), F32), vmem((tk, tq), F32), vmem((tk, tq), F32), vmem((tk, tq), F32),
                        vmem((1, tq), F32), vmem((1, tq), F32), vmem((1, tq), F32), vmem((1, tq), F32),
                        vmem((ACC_ROWS, tq), F32), vmem((ACC_ROWS, tq), F32)],
        compiler_params=pltpu.CompilerParams(dimension_semantics=("arbitrary", "arbitrary"),
                                             vmem_limit_bytes=VMEM_LIMIT),
        name="diff_attention",
    )(qt, ka, ka, vt, mk, mv, lam_vecs, g)


def _meta_attn_body(q_ref, k_ref, v_ref, slope_ref, lam_ref, g_ref, o_ref, *, lam_init):
    q = q_ref[...]
    k = k_ref[...]
    n = q.shape[0]
    lane = lax.broadcasted_iota(jnp.int32, q.shape, 1)
    zero = jnp.zeros_like(q)
    contract_last = (((1,), (1,)), ((), ()))
    s1 = lax.dot_general(jnp.where(lane < HEAD_DIM, q, zero), k, contract_last,
                         preferred_element_type=F32)
    s2 = lax.dot_general(jnp.where(lane >= HEAD_DIM, q, zero), k, contract_last,
                         preferred_element_type=F32)
    qpos = lax.broadcasted_iota(jnp.int32, (n, n), 0)
    kpos = lax.broadcasted_iota(jnp.int32, (n, n), 1)
    bias = -slope_ref[...][:, 0:1] * (qpos - kpos).astype(F32)

    def softmax(s):
        s = jnp.where(kpos <= qpos, s + bias, NEG)
        e = jnp.exp(s - jnp.max(s, axis=-1, keepdims=True))
        return e / jnp.sum(e, axis=-1, keepdims=True)

    w = softmax(s1) - _lam(lam_ref, lam_init) * softmax(s2)
    o_ref[...] = _subln(_dot(w.astype(BF16), v_ref[...]), g_ref[...], lam_init).astype(BF16)


def _meta_attention(q, k, v, slopes, lam_vecs, g, *, lam_init):
    full = lambda a: pl.BlockSpec(a.shape, lambda h: (0,) * a.ndim)
    head = pl.BlockSpec((N_META, V_DIM), lambda h: (0, h))
    return pl.pallas_call(
        functools.partial(_meta_attn_body, lam_init=lam_init),
        grid=(N_HEADS,),
        in_specs=[head, head, head, pl.BlockSpec((None, 1, LANES), lambda h: (h, 0, 0)),
                  full(lam_vecs), full(g)],
        out_specs=head,
        out_shape=jax.ShapeDtypeStruct(q.shape, BF16),
        name="meta_attention",
    )(q, k, v, slopes, lam_vecs, g)


def _merge_body(ypre_ref, o_ref, sga_ref, sgb_ref, h_ref, wc_ref, wa_ref, wm_ref, g_ref, out_ref):
    ya = _dot(ypre_ref[...], wc_ref[...])
    yb = _dot(o_ref[...], wa_ref[...])
    gated = sga_ref[...].astype(F32) * ya + sgb_ref[...].astype(F32) * yb
    mix = _dot(gated.astype(BF16), wm_ref[...])
    out_ref[...] = h_ref[...] + _rms(mix, g_ref[...])


def _merge(ypre, o, sga, sgb, h, wc, wa, wm, g):
    rows, d = h.shape
    tm = min(ROW_TILE, rows)
    assert rows % tm == 0
    row_spec = lambda n: pl.BlockSpec((tm, n), lambda i: (i, 0))
    return pl.pallas_call(
        _merge_body,
        grid=(rows // tm,),
        in_specs=[row_spec(ypre.shape[1]), row_spec(o.shape[1]), row_spec(sga.shape[1]),
                  row_spec(sgb.shape[1]), row_spec(d),
                  _resident(wc.shape), _resident(wa.shape), _resident(wm.shape), _resident(g.shape)],
        out_specs=row_spec(d),
        out_shape=jax.ShapeDtypeStruct((rows, d), F32),
        compiler_params=pltpu.CompilerParams(dimension_semantics=("arbitrary",),
                                             vmem_limit_bytes=VMEM_LIMIT),
        name="merge",
    )(ypre, o, sga, sgb, h, wc, wa, wm, g)


def _ffn_body(h_ref, gpre_ref, wup_ref, cw_ref, cb_ref, hist_ref, wdn_ref, gpost_ref,
              out_ref, tail_ref, f_ref, buf_ref, carry_ref, acc_ref, *, tiles_per_seq, d_ff):
    @pl.when(pl.program_id(0) % tiles_per_seq == 0)
    def _():
        carry_ref[...] = hist_ref[...]

    f_ref[...] = _rms(h_ref[...], gpre_ref[...]).astype(BF16)

    n_chunks = d_ff // COL_CHUNK
    half_cols = lambda c, half: slice(half * d_ff + c * COL_CHUNK, half * d_ff + (c + 1) * COL_CHUNK)
    half_buf = lambda c, half: buf_ref.at[(2 * c + half) % CONV_BUFFERS]

    def up(c):
        for half in range(2):
            cols = half_cols(c, half)
            _conv_stage(_dot(f_ref[...], wup_ref[:, cols]), half_buf(c, half), carry_ref, cols)

    def gate_down(c):
        g, u = (_conv_apply(cw_ref[:, half_cols(c, half)], half_buf(c, half)) + cb_ref[:, half_cols(c, half)]
                for half in range(2))
        a = (jax.nn.silu(g) * u).astype(BF16)
        return _dot(a, wdn_ref[c * COL_CHUNK:(c + 1) * COL_CHUNK, :])

    up(0)
    for c in range(n_chunks):
        if c + 1 < n_chunks:
            up(c + 1)
        y = gate_down(c)
        if c == 0:
            acc_ref[...] = y
        else:
            acc_ref[...] += y

    tail_ref[...] = carry_ref[...]
    out_ref[...] = h_ref[...] + _rms(acc_ref[...], gpost_ref[...])


def _ffn(h, gpre, wup, conv_w, conv_b, hist, wdn, gpost, *, seq_rows):
    rows, d = h.shape
    d_ff = wdn.shape[0]
    tm = min(ROW_TILE, rows)
    assert rows % tm == 0 and seq_rows % tm == 0 and d_ff % COL_CHUNK == 0
    row_spec = pl.BlockSpec((tm, d), lambda i: (i, 0))
    return pl.pallas_call(
        functools.partial(_ffn_body, tiles_per_seq=seq_rows // tm, d_ff=d_ff),
        grid=(rows // tm,),
        in_specs=[row_spec, _resident(gpre.shape), _resident(wup.shape), _resident(conv_w.shape),
                  _resident(conv_b.shape), _resident(hist.shape), _resident(wdn.shape),
                  _resident(gpost.shape)],
        out_specs=[row_spec, pl.BlockSpec((SUBLANES, 2 * d_ff), lambda i: (0, 0))],
        out_shape=[jax.ShapeDtypeStruct((rows, d), F32),
                   jax.ShapeDtypeStruct((SUBLANES, 2 * d_ff), F32)],
        scratch_shapes=[pltpu.VMEM((tm, d), BF16),
                        pltpu.VMEM((CONV_BUFFERS, tm + SUBLANES, COL_CHUNK), F32),
                        pltpu.VMEM((SUBLANES, 2 * d_ff), F32),
                        pltpu.VMEM((tm, d), F32)],
        compiler_params=pltpu.CompilerParams(dimension_semantics=("arbitrary",),
                                             vmem_limit_bytes=VMEM_LIMIT),
        name="ffn",
    )(h, gpre, wup, conv_w, conv_b, hist, wdn, gpost)


def kernel(x, meta_tokens, w_in, conv_w, w_conv_out, lambda_q1, lambda_k1, lambda_q2, lambda_k2,
           subln_g, w_attn_out, w_mix_out, norm_mix_pre, norm_mix_post, w_ffn_up, ffn_conv_w,
           ffn_conv_b, w_ffn_down, norm_ffn_pre, norm_ffn_post):
    batch, seq, d = x.shape
    depth = w_in.shape[0]
    assert meta_tokens.shape[0] == N_META and seq + N_META <= POS_SPLIT * 256 and seq % ROW_TILE == 0
    slopes = jnp.asarray(np.broadcast_to(_alibi_slopes()[:, None, None], (N_HEADS, 1, LANES)), F32)
    row = lambda a: a.reshape(1, -1)

    hx = x.reshape(batch * seq, d)
    hm = meta_tokens.astype(x.dtype)
    for l in range(depth):
        lam_init = 0.8 - 0.6 * math.exp(-0.3 * l)
        w_in_l = w_in[l].astype(BF16)
        lam_vecs = jnp.stack([lambda_q1[l], lambda_k1[l], lambda_q2[l], lambda_k2[l]]).astype(F32)
        g_sub = row(subln_g[l])
        zero_hist = jnp.zeros((SUBLANES, conv_w.shape[2]), F32)

        ypre_m, q_m, k_m, v_m, sga_m, sgb_m, u_tail = _in_proj(
            hm, row(norm_mix_pre[l]), w_in_l, conv_w[l], zero_hist, seq_rows=N_META, attn_layout=False)
        ypre, qt, ka, vt, sga, sgb, _ = _in_proj(
            hx, row(norm_mix_pre[l]), w_in_l, conv_w[l], u_tail, seq_rows=seq, attn_layout=True)

        o_m = _meta_attention(q_m, k_m, v_m, slopes, lam_vecs, g_sub, lam_init=lam_init)
        o = _attention(qt, ka, vt, k_m, v_m, lam_vecs, g_sub, batch=batch, seq=seq, lam_init=lam_init)

        wc, wa, wm = (w.astype(BF16) for w in (w_conv_out[l], w_attn_out[l], w_mix_out[l]))
        hm = _merge(ypre_m, o_m, sga_m, sgb_m, hm, wc, wa, wm, row(norm_mix_post[l]))
        hx = _merge(ypre, o, sga, sgb, hx, wc, wa, wm, row(norm_mix_post[l]))

        wup, wdn = w_ffn_up[l].astype(BF16), w_ffn_down[l].astype(BF16)
        zero_hist = jnp.zeros((SUBLANES, wup.shape[1]), F32)
        ffn = functools.partial(_ffn, gpre=row(norm_ffn_pre[l]), wup=wup, conv_w=ffn_conv_w[l],
                                conv_b=row(ffn_conv_b[l]), wdn=wdn, gpost=row(norm_ffn_post[l]))
        hm, z_tail = ffn(hm, hist=zero_hist, seq_rows=N_META)
        hx, _ = ffn(hx, hist=z_tail, seq_rows=seq)
    return hx.reshape(batch, seq, d)
```

```python
import functools
import math

import numpy as np
import jax
import jax.numpy as jnp
from jax import lax
from jax.experimental import pallas as pl
from jax.experimental.pallas import tpu as pltpu

F32 = jnp.float32
BF16 = jnp.bfloat16

N_META = 16
N_HEADS = 8
HEAD_DIM = 64
V_DIM = 2 * HEAD_DIM
N_SEG = 8
RMS_EPS = 1e-6
NEG = -1e30

LANES = 128
SUBLANES = 8
ROW_TILE = 512
COL_CHUNK = 256
CONV_BUFFERS = 4
Q_TILE = ROW_TILE
K_TILE = ROW_TILE
ACC_ROWS = V_DIM + 16
VMEM_LIMIT = 56 * 1024 * 1024

POS_SPLIT = 64
COEF_PARTS = 3
LANE_POS_HI = HEAD_DIM
LANE_POS_LO = HEAD_DIM + COEF_PARTS
LOG2E = math.log2(math.e)


def _rms(x, g):
    ms = jnp.mean(x * x, axis=-1, keepdims=True)
    return x * lax.rsqrt(ms + RMS_EPS) * g


def _dot(a, b):
    return jnp.dot(a, b, preferred_element_type=F32)


def _conv_stage(z, buf_ref, carry_ref, cols):
    tm = z.shape[0]
    buf_ref[0:SUBLANES, :] = carry_ref[:, cols]
    buf_ref[SUBLANES:SUBLANES + tm, :] = z
    carry_ref[:, cols] = z[tm - SUBLANES:tm]


def _conv_apply(w, buf_ref):
    tm = buf_ref.shape[0] - SUBLANES
    return (w[2:3] * buf_ref[SUBLANES:SUBLANES + tm, :]
            + w[1:2] * buf_ref[SUBLANES - 1:SUBLANES - 1 + tm, :]
            + w[0:1] * buf_ref[SUBLANES - 2:SUBLANES - 2 + tm, :])


def _causal_conv3(z, w, buf_ref, carry_ref, cols):
    _conv_stage(z, buf_ref, carry_ref, cols)
    return _conv_apply(w, buf_ref)


def _alibi_slopes():
    return 2.0 ** (-8.0 * np.arange(1, N_HEADS + 1, dtype=np.float64) / N_HEADS)


def _bf16_parts(value):
    parts, rest = [], np.float64(value)
    for _ in range(COEF_PARTS):
        part = np.float64(rest.astype(BF16))
        parts.append(float(part))
        rest = rest - part
    return parts


def _lane_select(lane, first_lane, values, otherwise):
    out = otherwise
    for j, value in enumerate(values):
        out = jnp.where(lane == first_lane + j, value, out)
    return out


def _key_features(pos0, n):
    pos = lax.broadcasted_iota(jnp.int32, (n, LANES), 0) + pos0
    lane = lax.broadcasted_iota(jnp.int32, (n, LANES), 1)
    hi = (pos >> 6).astype(F32)
    lo = (pos & (POS_SPLIT - 1)).astype(F32)
    in_hi = (lane >= LANE_POS_HI) & (lane < LANE_POS_HI + COEF_PARTS)
    in_lo = (lane >= LANE_POS_LO) & (lane < LANE_POS_LO + COEF_PARTS)
    return jnp.where(in_hi, hi, jnp.where(in_lo, lo, 0.0)), lane


def _query_features(slope, lane):
    parts = _bf16_parts(slope * LOG2E)
    feat = _lane_select(lane, LANE_POS_HI, [p * POS_SPLIT for p in parts], 0.0)
    return _lane_select(lane, LANE_POS_LO, parts, feat)


def _split_maps(x, feat, lane):
    first = jnp.where(lane < HEAD_DIM, x, feat)
    second = jnp.where(lane < HEAD_DIM, pltpu.roll(x, HEAD_DIM, axis=1), feat)
    return first, second


def _ones_rows(n):
    return jnp.where(lax.broadcasted_iota(jnp.int32, (ACC_ROWS - V_DIM, n), 0) == 0, 1.0, 0.0)


def _in_proj_body(x_ref, g_ref, w_ref, cw_ref, hist_ref,
                  ypre_ref, q_ref, k_ref, v_ref, sga_ref, sgb_ref, tail_ref,
                  xn_ref, buf_ref, carry_ref, *, tiles_per_seq, width, q_scale, attn_layout):
    tile_in_seq = pl.program_id(0) % tiles_per_seq

    @pl.when(tile_in_seq == 0)
    def _():
        carry_ref[...] = hist_ref[...]

    xn_ref[...] = _rms(x_ref[...], g_ref[...]).astype(BF16)
    tm = x_ref.shape[0]
    heads_per_chunk = COL_CHUNK // V_DIM
    if attn_layout:
        slopes = _alibi_slopes()
        kfeat, lane = _key_features(N_META + tile_in_seq * tm, tm)

    for c in range(width // COL_CHUNK):
        cols = slice(c * COL_CHUNK, (c + 1) * COL_CHUNK)

        def proj(seg):
            lo = seg * width + c * COL_CHUNK
            return _dot(xn_ref[...], w_ref[:, lo:lo + COL_CHUNK])

        u = proj(1) * proj(2)
        y = _causal_conv3(u, cw_ref[:, cols], buf_ref.at[c % CONV_BUFFERS], carry_ref, cols)
        ypre_ref[:, cols] = (proj(0) * y).astype(BF16)
        sga_ref[:, cols] = jax.nn.sigmoid(proj(6)).astype(BF16)
        sgb_ref[:, cols] = jax.nn.sigmoid(proj(7)).astype(BF16)
        q, k, v = proj(3) * q_scale, proj(4), proj(5)
        if not attn_layout:
            q_ref[:, cols] = q.astype(BF16)
            k_ref[:, cols] = k.astype(BF16)
            v_ref[:, cols] = v.astype(BF16)
            continue
        for hh in range(heads_per_chunk):
            head = c * heads_per_chunk + hh
            lanes = slice(hh * V_DIM, (hh + 1) * V_DIM)
            qfeat = _query_features(float(slopes[head]), lane)
            for m, qm in enumerate(_split_maps(q[:, lanes], qfeat, lane)):
                q_ref[head, m, 0] = qm.T.astype(BF16)
            for m, km in enumerate(_split_maps(k[:, lanes], kfeat, lane)):
                k_ref[:, (2 * head + m) * LANES:(2 * head + m + 1) * LANES] = km.astype(BF16)
            v_ref[head, 0, 0:V_DIM, :] = v[:, lanes].T.astype(BF16)
            v_ref[head, 0, V_DIM:ACC_ROWS, :] = _ones_rows(tm).astype(BF16)

    tail_ref[...] = carry_ref[...]


def _resident(shape):
    return pl.BlockSpec(shape, lambda i: (0,) * len(shape), pipeline_mode=pl.Buffered(1))


def _in_proj(h, gain, w_in, conv_w, hist, *, seq_rows, attn_layout):
    rows, d = h.shape
    width = w_in.shape[1] // N_SEG
    tm = min(ROW_TILE, rows)
    n_tiles = rows // tm
    assert rows % tm == 0 and seq_rows % tm == 0 and width % COL_CHUNK == 0
    assert width == N_HEADS * V_DIM and COL_CHUNK % V_DIM == 0
    row_spec = lambda n: pl.BlockSpec((tm, n), lambda i: (i, 0))
    act = jax.ShapeDtypeStruct((rows, width), BF16)
    if attn_layout:
        qkv_specs = [pl.BlockSpec((N_HEADS, 2, 1, LANES, tm), lambda i: (0, 0, i, 0, 0)),
                     row_spec(2 * width),
                     pl.BlockSpec((N_HEADS, 1, ACC_ROWS, tm), lambda i: (0, i, 0, 0))]
        qkv_shapes = [jax.ShapeDtypeStruct((N_HEADS, 2, n_tiles, LANES, tm), BF16),
                      jax.ShapeDtypeStruct((rows, 2 * width), BF16),
                      jax.ShapeDtypeStruct((N_HEADS, n_tiles, ACC_ROWS, tm), BF16)]
    else:
        qkv_specs, qkv_shapes = [row_spec(width)] * 3, [act] * 3
    return pl.pallas_call(
        functools.partial(_in_proj_body, tiles_per_seq=seq_rows // tm, width=width,
                          q_scale=HEAD_DIM ** -0.5 * (LOG2E if attn_layout else 1.0),
                          attn_layout=attn_layout),
        grid=(n_tiles,),
        in_specs=[row_spec(d), _resident((1, d)), _resident(w_in.shape), _resident(conv_w.shape),
                  _resident(hist.shape)],
        out_specs=[row_spec(width)] + qkv_specs + [row_spec(width)] * 2
                  + [pl.BlockSpec((SUBLANES, width), lambda i: (0, 0))],
        out_shape=[act] + qkv_shapes + [act] * 2 + [jax.ShapeDtypeStruct((SUBLANES, width), F32)],
        scratch_shapes=[pltpu.VMEM((tm, d), BF16),
                        pltpu.VMEM((CONV_BUFFERS, tm + SUBLANES, COL_CHUNK), F32),
                        pltpu.VMEM((SUBLANES, width), F32)],
        compiler_params=pltpu.CompilerParams(dimension_semantics=("arbitrary",),
                                             vmem_limit_bytes=VMEM_LIMIT),
        name="in_proj",
    )(h, gain, w_in, conv_w, hist)


def _lam(lam_ref, lam_init):
    l = lam_ref[...]
    return (jnp.exp(jnp.sum(l[0:1] * l[1:2], axis=-1, keepdims=True))
            - jnp.exp(jnp.sum(l[2:3] * l[3:4], axis=-1, keepdims=True)) + lam_init)


def _subln(o, g, lam_init):
    return _rms(o, g) * (1.0 - lam_init)


def _attn_body(qt_ref, ka1_ref, ka2_ref, vt_ref, mk_ref, mv_ref, lam_ref, g_ref, o_ref,
               mask_ref, m01_ref, m02_ref, acc01_ref, acc02_ref,
               s1a_ref, s1b_ref, s2a_ref, s2b_ref, tm1a_ref, tm1b_ref, tm2a_ref, tm2b_ref,
               acc1_ref, acc2_ref, *, lam_init):
    _, n_q, _, tq = qt_ref.shape
    n_k, _, tk = vt_ref.shape
    s1_ref, s2_ref = (s1a_ref, s1b_ref), (s2a_ref, s2b_ref)
    tmax1_ref, tmax2_ref = (tm1a_ref, tm1b_ref), (tm2a_ref, tm2b_ref)
    ratio = tq // tk
    n_items = ratio * n_q * (n_q + 1) // 2

    @pl.when((pl.program_id(0) == 0) & (pl.program_id(1) == 0))
    def _build_masks():
        kk = lax.broadcasted_iota(jnp.int32, (tk, tq), 0)
        qq = lax.broadcasted_iota(jnp.int32, (tk, tq), 1)
        for d in range(ratio):
            mask_ref[d] = jnp.where(kk + d * tk <= qq, 0.0, NEG)

    feat, lane = _key_features(0, N_META)
    mk1, mk2 = (a.astype(BF16) for a in _split_maps(mk_ref[...].astype(F32), feat, lane))
    mv_pad = jnp.concatenate([mv_ref[...].astype(F32), jnp.zeros((LANES - N_META, LANES), F32)], axis=0)
    mvt = jnp.concatenate([mv_pad.T[:, 0:N_META], _ones_rows(N_META)], axis=0).astype(BF16)

    pairs = [(i, m) for i in range(n_q) for m in range(2)]
    m0_refs, acc0_refs = (m01_ref, m02_ref), (acc01_ref, acc02_ref)
    s_meta = [_dot((mk1, mk2)[m], qt_ref[m, i]) for i, m in pairs]
    m0 = [jnp.max(s, axis=0, keepdims=True) for s in s_meta]
    p_meta = [jnp.exp2(s - mx).astype(BF16) for s, mx in zip(s_meta, m0)]
    for (i, m), mx, p in zip(pairs, m0, p_meta):
        m0_refs[m][i] = mx
        acc0_refs[m][i] = _dot(mvt, p)

    for par in range(min(2, n_q)):
        acc1_ref[par] = acc01_ref[par]
        acc2_ref[par] = acc02_ref[par]

    def stage_a(qi, t, slot, masked):
        r0 = pl.multiple_of(t * tk, tk)
        for m, (ka_ref, s_ref, tmax_ref) in enumerate(((ka1_ref, s1_ref, tmax1_ref),
                                                       (ka2_ref, s2_ref, tmax2_ref))):
            s = _dot(ka_ref[pl.ds(r0, tk), :], qt_ref[m, qi])
            if masked:
                s = s + mask_ref[jnp.maximum(t - qi * ratio, 0)]
            s_ref[slot][...] = s
            tmax_ref[slot][...] = jnp.max(s, axis=0, keepdims=True)

    def stage_b(qi, t, slot, ms):
        first = t == 0
        par = qi & 1
        out = []
        for s_ref, tmax_ref, acc_ref, m0_ref, m in ((s1_ref, tmax1_ref, acc1_ref, m01_ref, ms[0]),
                                                    (s2_ref, tmax2_ref, acc2_ref, m02_ref, ms[1])):
            m_old = jnp.where(first, m0_ref[qi], m)
            m_new = jnp.maximum(m_old, tmax_ref[slot][...])
            p = jnp.exp2(s_ref[slot][...] - m_new).astype(BF16)
            acc_ref[par] = jnp.exp2(m_old - m_new) * acc_ref[par] + _dot(vt_ref[t], p)
            out.append(m_new)
        return tuple(out)

    def finish_query_tile(qi):
        par = qi & 1
        w1 = 1.0 / acc1_ref[par, V_DIM:V_DIM + 1, :]
        w2 = _lam(lam_ref, lam_init) / acc2_ref[par, V_DIM:V_DIM + 1, :]
        o_t = acc1_ref[par, 0:V_DIM, :] * w1 - acc2_ref[par, 0:V_DIM, :] * w2
        scale = lax.rsqrt(jnp.mean(o_t * o_t, axis=0, keepdims=True) + RMS_EPS) * (1.0 - lam_init)
        r0 = pl.multiple_of(qi * tq, tq)
        o_ref[pl.ds(r0, tq), :] = ((o_t * scale).T * g_ref[...]).astype(BF16)
        nxt = jnp.minimum(qi + 2, n_q - 1)
        acc1_ref[par] = acc01_ref[nxt]
        acc2_ref[par] = acc02_ref[nxt]

    def advance(qi, t):
        end = t + 1 == (qi + 1) * ratio
        more = qi + 1 < n_q
        return (jnp.where(end & more, qi + 1, qi),
                jnp.where(end, jnp.where(more, 0, t), t + 1))

    def iteration(slot_b, carry):
        qa, ta, qb, tb, m1, m2 = carry

        def run(masked, finish_previous):
            def f():
                if finish_previous:
                    finish_query_tile(qb - 1)
                stage_a(qa, ta, 1 - slot_b, masked)
                return stage_b(qb, tb, slot_b, (m1, m2))
            return f

        masked = ta >= qa * ratio
        m1, m2 = lax.cond((tb == 0) & (qb > 0),
                          lambda: lax.cond(masked, run(True, True), run(False, True)),
                          lambda: lax.cond(masked, run(True, False), run(False, False)))
        qn, tn = advance(qa, ta)
        return qn, tn, qa, ta, m1, m2

    zero = jnp.int32(0)
    stage_a(zero, zero, 0, True)
    q1_, t1_ = advance(zero, zero)
    m_init = jnp.full((1, tq), NEG, F32)
    carry = lax.fori_loop(0, n_items // 2, lambda i, c: iteration(1, iteration(0, c)),
                          (q1_, t1_, zero, zero, m_init, m_init))
    if n_items % 2:
        iteration(0, carry)
    finish_query_tile(jnp.int32(n_q - 1))


def _attention(qt, ka, vt, mk, mv, lam_vecs, g, *, batch, seq, lam_init):
    tq, tk = qt.shape[-1], vt.shape[-1]
    n_q, n_k = seq // tq, seq // tk
    assert seq % tq == 0 and tq % tk == 0 and tk % LANES == 0
    assert qt.shape[2] == batch * n_q and vt.shape[1] == batch * n_k
    full = lambda a: pl.BlockSpec(a.shape, lambda b, h: (0,) * a.ndim)
    meta_spec = pl.BlockSpec((N_META, V_DIM), lambda b, h: (0, h))
    key_spec = lambda m: pl.BlockSpec((seq, LANES), lambda b, h: (b, 2 * h + m))
    vmem = pltpu.VMEM
    return pl.pallas_call(
        functools.partial(_attn_body, lam_init=lam_init),
        grid=(batch, N_HEADS),
        in_specs=[pl.BlockSpec((None, 2, n_q, LANES, tq), lambda b, h: (h, 0, b, 0, 0)),
                  key_spec(0), key_spec(1),
                  pl.BlockSpec((None, n_k, ACC_ROWS, tk), lambda b, h: (h, b, 0, 0)),
                  meta_spec, meta_spec, full(lam_vecs), full(g)],
        out_specs=pl.BlockSpec((seq, V_DIM), lambda b, h: (b, h)),
        out_shape=jax.ShapeDtypeStruct((batch * seq, N_HEADS * V_DIM), BF16),
        scratch_shapes=[vmem((tq // tk, tk, tq), F32),
                        vmem((n_q, 1, tq), F32), vmem((n_q, 1, tq), F32),
                        vmem((n_q, ACC_ROWS, tq), F32), vmem((n_q, ACC_ROWS, tq), F32),
                        vmem((tk, tq), F32), vmem((tk, tq), F32), vmem((tk, tq), F32), vmem((tk, tq), F32),
                        vmem((1, tq), F32), vmem((1, tq), F32), vmem((1, tq), F32), vmem((1, tq), F32),
                        vmem((2, ACC_ROWS, tq), F32), vmem((2, ACC_ROWS, tq), F32)],
        compiler_params=pltpu.CompilerParams(dimension_semantics=("arbitrary", "arbitrary"),
                                             vmem_limit_bytes=VMEM_LIMIT),
        name="diff_attention",
    )(qt, ka, ka, vt, mk, mv, lam_vecs, g)


def _meta_attn_body(q_ref, k_ref, v_ref, slope_ref, lam_ref, g_ref, o_ref, *, lam_init):
    q = q_ref[...]
    k = k_ref[...]
    n = q.shape[0]
    lane = lax.broadcasted_iota(jnp.int32, q.shape, 1)
    zero = jnp.zeros_like(q)
    contract_last = (((1,), (1,)), ((), ()))
    s1 = lax.dot_general(jnp.where(lane < HEAD_DIM, q, zero), k, contract_last,
                         preferred_element_type=F32)
    s2 = lax.dot_general(jnp.where(lane >= HEAD_DIM, q, zero), k, contract_last,
                         preferred_element_type=F32)
    qpos = lax.broadcasted_iota(jnp.int32, (n, n), 0)
    kpos = lax.broadcasted_iota(jnp.int32, (n, n), 1)
    bias = -slope_ref[...][:, 0:1] * (qpos - kpos).astype(F32)

    def softmax(s):
        s = jnp.where(kpos <= qpos, s + bias, NEG)
        e = jnp.exp(s - jnp.max(s, axis=-1, keepdims=True))
        return e / jnp.sum(e, axis=-1, keepdims=True)

    w = softmax(s1) - _lam(lam_ref, lam_init) * softmax(s2)
    o_ref[...] = _subln(_dot(w.astype(BF16), v_ref[...]), g_ref[...], lam_init).astype(BF16)


def _meta_attention(q, k, v, slopes, lam_vecs, g, *, lam_init):
    full = lambda a: pl.BlockSpec(a.shape, lambda h: (0,) * a.ndim)
    head = pl.BlockSpec((N_META, V_DIM), lambda h: (0, h))
    return pl.pallas_call(
        functools.partial(_meta_attn_body, lam_init=lam_init),
        grid=(N_HEADS,),
        in_specs=[head, head, head, pl.BlockSpec((None, 1, LANES), lambda h: (h, 0, 0)),
                  full(lam_vecs), full(g)],
        out_specs=head,
        out_shape=jax.ShapeDtypeStruct(q.shape, BF16),
        name="meta_attention",
    )(q, k, v, slopes, lam_vecs, g)


def _merge_body(ypre_ref, o_ref, sga_ref, sgb_ref, h_ref, wc_ref, wa_ref, wm_ref, g_ref, out_ref):
    ya = _dot(ypre_ref[...], wc_ref[...])
    yb = _dot(o_ref[...], wa_ref[...])
    gated = sga_ref[...].astype(F32) * ya + sgb_ref[...].astype(F32) * yb
    mix = _dot(gated.astype(BF16), wm_ref[...])
    out_ref[...] = h_ref[...] + _rms(mix, g_ref[...])


def _merge(ypre, o, sga, sgb, h, wc, wa, wm, g):
    rows, d = h.shape
    tm = min(ROW_TILE, rows)
    assert rows % tm == 0
    row_spec = lambda n: pl.BlockSpec((tm, n), lambda i: (i, 0))
    return pl.pallas_call(
        _merge_body,
        grid=(rows // tm,),
        in_specs=[row_spec(ypre.shape[1]), row_spec(o.shape[1]), row_spec(sga.shape[1]),
                  row_spec(sgb.shape[1]), row_spec(d),
                  _resident(wc.shape), _resident(wa.shape), _resident(wm.shape), _resident(g.shape)],
        out_specs=row_spec(d),
        out_shape=jax.ShapeDtypeStruct((rows, d), F32),
        compiler_params=pltpu.CompilerParams(dimension_semantics=("arbitrary",),
                                             vmem_limit_bytes=VMEM_LIMIT),
        name="merge",
    )(ypre, o, sga, sgb, h, wc, wa, wm, g)


def _ffn_body(h_ref, gpre_ref, wup_ref, cw_ref, cb_ref, hist_ref, wdn_ref, gpost_ref,
              out_ref, tail_ref, f_ref, buf_ref, carry_ref, acc_ref, *, tiles_per_seq, d_ff):
    @pl.when(pl.program_id(0) % tiles_per_seq == 0)
    def _():
        carry_ref[...] = hist_ref[...]

    f_ref[...] = _rms(h_ref[...], gpre_ref[...]).astype(BF16)

    n_chunks = d_ff // COL_CHUNK
    half_cols = lambda c, half: slice(half * d_ff + c * COL_CHUNK, half * d_ff + (c + 1) * COL_CHUNK)
    half_buf = lambda c, half: buf_ref.at[(2 * c + half) % CONV_BUFFERS]

    def up(c):
        for half in range(2):
            cols = half_cols(c, half)
            _conv_stage(_dot(f_ref[...], wup_ref[:, cols]), half_buf(c, half), carry_ref, cols)

    def gate_down(c):
        g, u = (_conv_apply(cw_ref[:, half_cols(c, half)], half_buf(c, half)) + cb_ref[:, half_cols(c, half)]
                for half in range(2))
        a = (jax.nn.silu(g) * u).astype(BF16)
        return _dot(a, wdn_ref[c * COL_CHUNK:(c + 1) * COL_CHUNK, :])

    up(0)
    for c in range(n_chunks):
        if c + 1 < n_chunks:
            up(c + 1)
        y = gate_down(c)
        if c == 0:
            acc_ref[...] = y
        else:
            acc_ref[...] += y

    tail_ref[...] = carry_ref[...]
    out_ref[...] = h_ref[...] + _rms(acc_ref[...], gpost_ref[...])


def _ffn(h, gpre, wup, conv_w, conv_b, hist, wdn, gpost, *, seq_rows):
    rows, d = h.shape
    d_ff = wdn.shape[0]
    tm = min(ROW_TILE, rows)
    assert rows % tm == 0 and seq_rows % tm == 0 and d_ff % COL_CHUNK == 0
    row_spec = pl.BlockSpec((tm, d), lambda i: (i, 0))
    return pl.pallas_call(
        functools.partial(_ffn_body, tiles_per_seq=seq_rows // tm, d_ff=d_ff),
        grid=(rows // tm,),
        in_specs=[row_spec, _resident(gpre.shape), _resident(wup.shape), _resident(conv_w.shape),
                  _resident(conv_b.shape), _resident(hist.shape), _resident(wdn.shape),
                  _resident(gpost.shape)],
        out_specs=[row_spec, pl.BlockSpec((SUBLANES, 2 * d_ff), lambda i: (0, 0))],
        out_shape=[jax.ShapeDtypeStruct((rows, d), F32),
                   jax.ShapeDtypeStruct((SUBLANES, 2 * d_ff), F32)],
        scratch_shapes=[pltpu.VMEM((tm, d), BF16),
                        pltpu.VMEM((CONV_BUFFERS, tm + SUBLANES, COL_CHUNK), F32),
                        pltpu.VMEM((SUBLANES, 2 * d_ff), F32),
                        pltpu.VMEM((tm, d), F32)],
        compiler_params=pltpu.CompilerParams(dimension_semantics=("arbitrary",),
                                             vmem_limit_bytes=VMEM_LIMIT),
        name="ffn",
    )(h, gpre, wup, conv_w, conv_b, hist, wdn, gpost)


def kernel(x, meta_tokens, w_in, conv_w, w_conv_out, lambda_q1, lambda_k1, lambda_q2, lambda_k2,
           subln_g, w_attn_out, w_mix_out, norm_mix_pre, norm_mix_post, w_ffn_up, ffn_conv_w,
           ffn_conv_b, w_ffn_down, norm_ffn_pre, norm_ffn_post):
    batch, seq, d = x.shape
    depth = w_in.shape[0]
    assert meta_tokens.shape[0] == N_META and seq + N_META <= POS_SPLIT * 256 and seq % ROW_TILE == 0
    slopes = jnp.asarray(np.broadcast_to(_alibi_slopes()[:, None, None], (N_HEADS, 1, LANES)), F32)
    row = lambda a: a.reshape(1, -1)

    hx = x.reshape(batch * seq, d)
    hm = meta_tokens.astype(x.dtype)
    for l in range(depth):
        lam_init = 0.8 - 0.6 * math.exp(-0.3 * l)
        w_in_l = w_in[l].astype(BF16)
        lam_vecs = jnp.stack([lambda_q1[l], lambda_k1[l], lambda_q2[l], lambda_k2[l]]).astype(F32)
        g_sub = row(subln_g[l])
        zero_hist = jnp.zeros((SUBLANES, conv_w.shape[2]), F32)

        ypre_m, q_m, k_m, v_m, sga_m, sgb_m, u_tail = _in_proj(
            hm, row(norm_mix_pre[l]), w_in_l, conv_w[l], zero_hist, seq_rows=N_META, attn_layout=False)
        ypre, qt, ka, vt, sga, sgb, _ = _in_proj(
            hx, row(norm_mix_pre[l]), w_in_l, conv_w[l], u_tail, seq_rows=seq, attn_layout=True)

        o_m = _meta_attention(q_m, k_m, v_m, slopes, lam_vecs, g_sub, lam_init=lam_init)
        o = _attention(qt, ka, vt, k_m, v_m, lam_vecs, g_sub, batch=batch, seq=seq, lam_init=lam_init)

        wc, wa, wm = (w.astype(BF16) for w in (w_conv_out[l], w_attn_out[l], w_mix_out[l]))
        hm = _merge(ypre_m, o_m, sga_m, sgb_m, hm, wc, wa, wm, row(norm_mix_post[l]))
        hx = _merge(ypre, o, sga, sgb, hx, wc, wa, wm, row(norm_mix_post[l]))

        wup, wdn = w_ffn_up[l].astype(BF16), w_ffn_down[l].astype(BF16)
        zero_hist = jnp.zeros((SUBLANES, wup.shape[1]), F32)
        ffn = functools.partial(_ffn, gpre=row(norm_ffn_pre[l]), wup=wup, conv_w=ffn_conv_w[l],
                                conv_b=row(ffn_conv_b[l]), wdn=wdn, gpost=row(norm_ffn_post[l]))
        hm, z_tail = ffn(hm, hist=zero_hist, seq_rows=N_META)
        hx, _ = ffn(hx, hist=z_tail, seq_rows=seq)
    return hx.reshape(batch, seq, d)
```

```python
import functools
import math

import numpy as np
import jax
import jax.numpy as jnp
from jax import lax
from jax.experimental import pallas as pl
from jax.experimental.pallas import tpu as pltpu

F32 = jnp.float32
BF16 = jnp.bfloat16

N_META = 16
N_HEADS = 8
HEAD_DIM = 64
V_DIM = 2 * HEAD_DIM
N_SEG = 8
RMS_EPS = 1e-6
NEG = -1e30

LANES = 128
SUBLANES = 8
ROW_TILE = 512
COL_CHUNK = 256
CONV_BUFFERS = 4
Q_TILE = ROW_TILE
K_TILE = ROW_TILE
ACC_ROWS = V_DIM + 16
HEADS_PER_STEP = 2
VMEM_LIMIT = 56 * 1024 * 1024

POS_SPLIT = 64
COEF_PARTS = 3
LANE_POS_HI = HEAD_DIM
LANE_POS_LO = HEAD_DIM + COEF_PARTS
LOG2E = math.log2(math.e)


def _rms(x, g):
    ms = jnp.mean(x * x, axis=-1, keepdims=True)
    return x * lax.rsqrt(ms + RMS_EPS) * g


def _dot(a, b):
    return jnp.dot(a, b, preferred_element_type=F32)


def _conv_stage(z, buf_ref, carry_ref, cols):
    tm = z.shape[0]
    buf_ref[0:SUBLANES, :] = carry_ref[:, cols]
    buf_ref[SUBLANES:SUBLANES + tm, :] = z
    carry_ref[:, cols] = z[tm - SUBLANES:tm]


def _conv_apply(w, buf_ref):
    tm = buf_ref.shape[0] - SUBLANES
    return (w[2:3] * buf_ref[SUBLANES:SUBLANES + tm, :]
            + w[1:2] * buf_ref[SUBLANES - 1:SUBLANES - 1 + tm, :]
            + w[0:1] * buf_ref[SUBLANES - 2:SUBLANES - 2 + tm, :])


def _causal_conv3(z, w, buf_ref, carry_ref, cols):
    _conv_stage(z, buf_ref, carry_ref, cols)
    return _conv_apply(w, buf_ref)


def _alibi_slopes():
    return 2.0 ** (-8.0 * np.arange(1, N_HEADS + 1, dtype=np.float64) / N_HEADS)


def _bf16_parts(value):
    parts, rest = [], np.float64(value)
    for _ in range(COEF_PARTS):
        part = np.float64(rest.astype(BF16))
        parts.append(float(part))
        rest = rest - part
    return parts


def _lane_select(lane, first_lane, values, otherwise):
    out = otherwise
    for j, value in enumerate(values):
        out = jnp.where(lane == first_lane + j, value, out)
    return out


def _key_features(pos0, n):
    pos = lax.broadcasted_iota(jnp.int32, (n, LANES), 0) + pos0
    lane = lax.broadcasted_iota(jnp.int32, (n, LANES), 1)
    hi = (pos >> 6).astype(F32)
    lo = (pos & (POS_SPLIT - 1)).astype(F32)
    in_hi = (lane >= LANE_POS_HI) & (lane < LANE_POS_HI + COEF_PARTS)
    in_lo = (lane >= LANE_POS_LO) & (lane < LANE_POS_LO + COEF_PARTS)
    return jnp.where(in_hi, hi, jnp.where(in_lo, lo, 0.0)), lane


def _query_features(slope, lane):
    parts = _bf16_parts(slope * LOG2E)
    feat = _lane_select(lane, LANE_POS_HI, [p * POS_SPLIT for p in parts], 0.0)
    return _lane_select(lane, LANE_POS_LO, parts, feat)


def _split_maps(x, feat, lane):
    first = jnp.where(lane < HEAD_DIM, x, feat)
    second = jnp.where(lane < HEAD_DIM, pltpu.roll(x, HEAD_DIM, axis=1), feat)
    return first, second


def _ones_rows(n):
    return jnp.where(lax.broadcasted_iota(jnp.int32, (ACC_ROWS - V_DIM, n), 0) == 0, 1.0, 0.0)


def _in_proj_body(x_ref, g_ref, w_ref, cw_ref, hist_ref,
                  ypre_ref, q_ref, k_ref, v_ref, sga_ref, sgb_ref, tail_ref,
                  xn_ref, buf_ref, carry_ref, *, tiles_per_seq, width, q_scale, attn_layout):
    tile_in_seq = pl.program_id(0) % tiles_per_seq

    @pl.when(tile_in_seq == 0)
    def _():
        carry_ref[...] = hist_ref[...]

    xn_ref[...] = _rms(x_ref[...], g_ref[...]).astype(BF16)
    tm = x_ref.shape[0]
    heads_per_chunk = COL_CHUNK // V_DIM
    if attn_layout:
        slopes = _alibi_slopes()
        kfeat, lane = _key_features(N_META + tile_in_seq * tm, tm)

    for c in range(width // COL_CHUNK):
        cols = slice(c * COL_CHUNK, (c + 1) * COL_CHUNK)

        def proj(seg):
            lo = seg * width + c * COL_CHUNK
            return _dot(xn_ref[...], w_ref[:, lo:lo + COL_CHUNK])

        u = proj(1) * proj(2)
        y = _causal_conv3(u, cw_ref[:, cols], buf_ref.at[c % CONV_BUFFERS], carry_ref, cols)
        ypre_ref[:, cols] = (proj(0) * y).astype(BF16)
        sga_ref[:, cols] = jax.nn.sigmoid(proj(6)).astype(BF16)
        sgb_ref[:, cols] = jax.nn.sigmoid(proj(7)).astype(BF16)
        q, k, v = proj(3) * q_scale, proj(4), proj(5)
        if not attn_layout:
            q_ref[:, cols] = q.astype(BF16)
            k_ref[:, cols] = k.astype(BF16)
            v_ref[:, cols] = v.astype(BF16)
            continue
        for hh in range(heads_per_chunk):
            head = c * heads_per_chunk + hh
            lanes = slice(hh * V_DIM, (hh + 1) * V_DIM)
            qfeat = _query_features(float(slopes[head]), lane)
            for m, qm in enumerate(_split_maps(q[:, lanes], qfeat, lane)):
                q_ref[head, m, 0] = qm.T.astype(BF16)
            for m, km in enumerate(_split_maps(k[:, lanes], kfeat, lane)):
                k_ref[:, (2 * head + m) * LANES:(2 * head + m + 1) * LANES] = km.astype(BF16)
            v_ref[head, 0, 0:V_DIM, :] = v[:, lanes].T.astype(BF16)
            v_ref[head, 0, V_DIM:ACC_ROWS, :] = _ones_rows(tm).astype(BF16)

    tail_ref[...] = carry_ref[...]


def _resident(shape):
    return pl.BlockSpec(shape, lambda i: (0,) * len(shape), pipeline_mode=pl.Buffered(1))


def _in_proj(h, gain, w_in, conv_w, hist, *, seq_rows, attn_layout):
    rows, d = h.shape
    width = w_in.shape[1] // N_SEG
    tm = min(ROW_TILE, rows)
    n_tiles = rows // tm
    assert rows % tm == 0 and seq_rows % tm == 0 and width % COL_CHUNK == 0
    assert width == N_HEADS * V_DIM and COL_CHUNK % V_DIM == 0
    row_spec = lambda n: pl.BlockSpec((tm, n), lambda i: (i, 0))
    act = jax.ShapeDtypeStruct((rows, width), BF16)
    if attn_layout:
        qkv_specs = [pl.BlockSpec((N_HEADS, 2, 1, LANES, tm), lambda i: (0, 0, i, 0, 0)),
                     row_spec(2 * width),
                     pl.BlockSpec((N_HEADS, 1, ACC_ROWS, tm), lambda i: (0, i, 0, 0))]
        qkv_shapes = [jax.ShapeDtypeStruct((N_HEADS, 2, n_tiles, LANES, tm), BF16),
                      jax.ShapeDtypeStruct((rows, 2 * width), BF16),
                      jax.ShapeDtypeStruct((N_HEADS, n_tiles, ACC_ROWS, tm), BF16)]
    else:
        qkv_specs, qkv_shapes = [row_spec(width)] * 3, [act] * 3
    return pl.pallas_call(
        functools.partial(_in_proj_body, tiles_per_seq=seq_rows // tm, width=width,
                          q_scale=HEAD_DIM ** -0.5 * (LOG2E if attn_layout else 1.0),
                          attn_layout=attn_layout),
        grid=(n_tiles,),
        in_specs=[row_spec(d), _resident((1, d)), _resident(w_in.shape), _resident(conv_w.shape),
                  _resident(hist.shape)],
        out_specs=[row_spec(width)] + qkv_specs + [row_spec(width)] * 2
                  + [pl.BlockSpec((SUBLANES, width), lambda i: (0, 0))],
        out_shape=[act] + qkv_shapes + [act] * 2 + [jax.ShapeDtypeStruct((SUBLANES, width), F32)],
        scratch_shapes=[pltpu.VMEM((tm, d), BF16),
                        pltpu.VMEM((CONV_BUFFERS, tm + SUBLANES, COL_CHUNK), F32),
                        pltpu.VMEM((SUBLANES, width), F32)],
        compiler_params=pltpu.CompilerParams(dimension_semantics=("arbitrary",),
                                             vmem_limit_bytes=VMEM_LIMIT),
        name="in_proj",
    )(h, gain, w_in, conv_w, hist)


def _lam(lam_ref, lam_init):
    l = lam_ref[...]
    return (jnp.exp(jnp.sum(l[0:1] * l[1:2], axis=-1, keepdims=True))
            - jnp.exp(jnp.sum(l[2:3] * l[3:4], axis=-1, keepdims=True)) + lam_init)


def _subln(o, g, lam_init):
    return _rms(o, g) * (1.0 - lam_init)


def _attn_body(qt_ref, ka_ref, vt_ref, mk_ref, mv_ref, lam_ref, g_ref, o_ref, mask_ref, *scratch,
               lam_init):
    n_heads, _, n_q, _, tq = qt_ref.shape
    n_k, _, tk = vt_ref.shape[1:]
    ratio = tq // tk
    n_items = ratio * n_q * (n_q + 1) // 2
    streams = [(hd, m) for hd in range(n_heads) for m in range(2)]
    m0_refs, acc0_refs, s_refs, tmax_refs, acc_refs = [], [], [], [], []
    for j in range(len(streams)):
        m0, acc0, s_a, s_b, tm_a, tm_b, acc = scratch[7 * j:7 * j + 7]
        m0_refs.append(m0)
        acc0_refs.append(acc0)
        s_refs.append((s_a, s_b))
        tmax_refs.append((tm_a, tm_b))
        acc_refs.append(acc)
    head_lanes = lambda hd: slice(hd * V_DIM, (hd + 1) * V_DIM)

    @pl.when((pl.program_id(0) == 0) & (pl.program_id(1) == 0))
    def _build_masks():
        kk = lax.broadcasted_iota(jnp.int32, (tk, tq), 0)
        qq = lax.broadcasted_iota(jnp.int32, (tk, tq), 1)
        for d in range(ratio):
            mask_ref[d] = jnp.where(kk + d * tk <= qq, 0.0, NEG)

    feat, lane = _key_features(0, N_META)
    mk_aug, mvt = [], []
    for hd in range(n_heads):
        mk_aug.extend(a.astype(BF16) for a in _split_maps(mk_ref[:, head_lanes(hd)].astype(F32), feat, lane))
        mv_pad = jnp.concatenate([mv_ref[:, head_lanes(hd)].astype(F32),
                                  jnp.zeros((LANES - N_META, LANES), F32)], axis=0)
        mvt.append(jnp.concatenate([mv_pad.T[:, 0:N_META], _ones_rows(N_META)], axis=0).astype(BF16))

    pairs = [(j, i) for j in range(len(streams)) for i in range(n_q)]
    s_meta = [_dot(mk_aug[j], qt_ref[streams[j][0], streams[j][1], i]) for j, i in pairs]
    m0 = [jnp.max(s, axis=0, keepdims=True) for s in s_meta]
    p_meta = [jnp.exp2(s - mx).astype(BF16) for s, mx in zip(s_meta, m0)]
    for (j, i), mx, p in zip(pairs, m0, p_meta):
        m0_refs[j][i] = mx
        acc0_refs[j][i] = _dot(mvt[streams[j][0]], p)

    for j in range(len(streams)):
        for par in range(min(2, n_q)):
            acc_refs[j][par] = acc0_refs[j][par]

    def stage_a(qi, t, slot, masked):
        r0 = pl.multiple_of(t * tk, tk)
        for j, (hd, m) in enumerate(streams):
            keys = ka_ref[pl.ds(r0, tk), (2 * hd + m) * LANES:(2 * hd + m + 1) * LANES]
            s = _dot(keys, qt_ref[hd, m, qi])
            if masked:
                s = s + mask_ref[jnp.maximum(t - qi * ratio, 0)]
            s_refs[j][slot][...] = s
            tmax_refs[j][slot][...] = jnp.max(s, axis=0, keepdims=True)

    def stage_b(qi, t, slot, ms):
        first = t == 0
        par = qi & 1
        out = []
        for j, (hd, m) in enumerate(streams):
            m_old = jnp.where(first, m0_refs[j][qi], ms[j])
            m_new = jnp.maximum(m_old, tmax_refs[j][slot][...])
            p = jnp.exp2(s_refs[j][slot][...] - m_new).astype(BF16)
            acc_refs[j][par] = jnp.exp2(m_old - m_new) * acc_refs[j][par] + _dot(vt_ref[hd, t], p)
            out.append(m_new)
        return tuple(out)

    def finish_query_tile(qi):
        par = qi & 1
        r0 = pl.multiple_of(qi * tq, tq)
        nxt = jnp.minimum(qi + 2, n_q - 1)
        for hd in range(n_heads):
            a1, a2 = acc_refs[2 * hd], acc_refs[2 * hd + 1]
            w1 = 1.0 / a1[par, V_DIM:V_DIM + 1, :]
            w2 = _lam(lam_ref, lam_init) / a2[par, V_DIM:V_DIM + 1, :]
            o_t = a1[par, 0:V_DIM, :] * w1 - a2[par, 0:V_DIM, :] * w2
            scale = lax.rsqrt(jnp.mean(o_t * o_t, axis=0, keepdims=True) + RMS_EPS) * (1.0 - lam_init)
            o_ref[pl.ds(r0, tq), head_lanes(hd)] = ((o_t * scale).T * g_ref[...]).astype(BF16)
            for j in (2 * hd, 2 * hd + 1):
                acc_refs[j][par] = acc0_refs[j][nxt]

    def advance(qi, t):
        end = t + 1 == (qi + 1) * ratio
        more = qi + 1 < n_q
        return (jnp.where(end & more, qi + 1, qi),
                jnp.where(end, jnp.where(more, 0, t), t + 1))

    def iteration(slot_b, carry):
        qa, ta, qb, tb, ms = carry

        def run(masked, finish_previous):
            def f():
                if finish_previous:
                    finish_query_tile(qb - 1)
                stage_a(qa, ta, 1 - slot_b, masked)
                return stage_b(qb, tb, slot_b, ms)
            return f

        masked = ta >= qa * ratio
        ms = lax.cond((tb == 0) & (qb > 0),
                      lambda: lax.cond(masked, run(True, True), run(False, True)),
                      lambda: lax.cond(masked, run(True, False), run(False, False)))
        qn, tn = advance(qa, ta)
        return qn, tn, qa, ta, ms

    zero = jnp.int32(0)
    stage_a(zero, zero, 0, True)
    q1_, t1_ = advance(zero, zero)
    m_init = tuple(jnp.full((1, tq), NEG, F32) for _ in streams)
    carry = lax.fori_loop(0, n_items // 2, lambda i, c: iteration(1, iteration(0, c)),
                          (q1_, t1_, zero, zero, m_init))
    if n_items % 2:
        iteration(0, carry)
    finish_query_tile(jnp.int32(n_q - 1))


def _attention(qt, ka, vt, mk, mv, lam_vecs, g, *, batch, seq, lam_init):
    tq, tk = qt.shape[-1], vt.shape[-1]
    n_q, n_k = seq // tq, seq // tk
    hp = HEADS_PER_STEP
    assert seq % tq == 0 and tq % tk == 0 and tk % LANES == 0 and N_HEADS % hp == 0
    assert qt.shape[2] == batch * n_q and vt.shape[1] == batch * n_k
    full = lambda a: pl.BlockSpec(a.shape, lambda b, h: (0,) * a.ndim)
    meta_spec = pl.BlockSpec((N_META, hp * V_DIM), lambda b, h: (0, h))
    vmem = pltpu.VMEM
    per_stream = [vmem((n_q, 1, tq), F32), vmem((n_q, ACC_ROWS, tq), F32),
                  vmem((tk, tq), F32), vmem((tk, tq), F32), vmem((1, tq), F32), vmem((1, tq), F32),
                  vmem((2, ACC_ROWS, tq), F32)]
    return pl.pallas_call(
        functools.partial(_attn_body, lam_init=lam_init),
        grid=(batch, N_HEADS // hp),
        in_specs=[pl.BlockSpec((hp, 2, n_q, LANES, tq), lambda b, h: (h, 0, b, 0, 0)),
                  pl.BlockSpec((seq, hp * 2 * LANES), lambda b, h: (b, h)),
                  pl.BlockSpec((hp, n_k, ACC_ROWS, tk), lambda b, h: (h, b, 0, 0)),
                  meta_spec, meta_spec, full(lam_vecs), full(g)],
        out_specs=pl.BlockSpec((seq, hp * V_DIM), lambda b, h: (b, h)),
        out_shape=jax.ShapeDtypeStruct((batch * seq, N_HEADS * V_DIM), BF16),
        scratch_shapes=[vmem((tq // tk, tk, tq), F32)] + per_stream * (2 * hp),
        compiler_params=pltpu.CompilerParams(dimension_semantics=("arbitrary", "arbitrary"),
                                             vmem_limit_bytes=VMEM_LIMIT),
        name="diff_attention",
    )(qt, ka, vt, mk, mv, lam_vecs, g)


def _meta_attn_body(q_ref, k_ref, v_ref, slope_ref, lam_ref, g_ref, o_ref, *, lam_init):
    q = q_ref[...]
    k = k_ref[...]
    n = q.shape[0]
    lane = lax.broadcasted_iota(jnp.int32, q.shape, 1)
    zero = jnp.zeros_like(q)
    contract_last = (((1,), (1,)), ((), ()))
    s1 = lax.dot_general(jnp.where(lane < HEAD_DIM, q, zero), k, contract_last,
                         preferred_element_type=F32)
    s2 = lax.dot_general(jnp.where(lane >= HEAD_DIM, q, zero), k, contract_last,
                         preferred_element_type=F32)
    qpos = lax.broadcasted_iota(jnp.int32, (n, n), 0)
    kpos = lax.broadcasted_iota(jnp.int32, (n, n), 1)
    bias = -slope_ref[...][:, 0:1] * (qpos - kpos).astype(F32)

    def softmax(s):
        s = jnp.where(kpos <= qpos, s + bias, NEG)
        e = jnp.exp(s - jnp.max(s, axis=-1, keepdims=True))
        return e / jnp.sum(e, axis=-1, keepdims=True)

    w = softmax(s1) - _lam(lam_ref, lam_init) * softmax(s2)
    o_ref[...] = _subln(_dot(w.astype(BF16), v_ref[...]), g_ref[...], lam_init).astype(BF16)


def _meta_attention(q, k, v, slopes, lam_vecs, g, *, lam_init):
    full = lambda a: pl.BlockSpec(a.shape, lambda h: (0,) * a.ndim)
    head = pl.BlockSpec((N_META, V_DIM), lambda h: (0, h))
    return pl.pallas_call(
        functools.partial(_meta_attn_body, lam_init=lam_init),
        grid=(N_HEADS,),
        in_specs=[head, head, head, pl.BlockSpec((None, 1, LANES), lambda h: (h, 0, 0)),
                  full(lam_vecs), full(g)],
        out_specs=head,
        out_shape=jax.ShapeDtypeStruct(q.shape, BF16),
        name="meta_attention",
    )(q, k, v, slopes, lam_vecs, g)


def _merge_body(ypre_ref, o_ref, sga_ref, sgb_ref, h_ref, wc_ref, wa_ref, wm_ref, g_ref, out_ref):
    ya = _dot(ypre_ref[...], wc_ref[...])
    yb = _dot(o_ref[...], wa_ref[...])
    gated = sga_ref[...].astype(F32) * ya + sgb_ref[...].astype(F32) * yb
    mix = _dot(gated.astype(BF16), wm_ref[...])
    out_ref[...] = h_ref[...] + _rms(mix, g_ref[...])


def _merge(ypre, o, sga, sgb, h, wc, wa, wm, g):
    rows, d = h.shape
    tm = min(ROW_TILE, rows)
    assert rows % tm == 0
    row_spec = lambda n: pl.BlockSpec((tm, n), lambda i: (i, 0))
    return pl.pallas_call(
        _merge_body,
        grid=(rows // tm,),
        in_specs=[row_spec(ypre.shape[1]), row_spec(o.shape[1]), row_spec(sga.shape[1]),
                  row_spec(sgb.shape[1]), row_spec(d),
                  _resident(wc.shape), _resident(wa.shape), _resident(wm.shape), _resident(g.shape)],
        out_specs=row_spec(d),
        out_shape=jax.ShapeDtypeStruct((rows, d), F32),
        compiler_params=pltpu.CompilerParams(dimension_semantics=("arbitrary",),
                                             vmem_limit_bytes=VMEM_LIMIT),
        name="merge",
    )(ypre, o, sga, sgb, h, wc, wa, wm, g)


def _ffn_body(h_ref, gpre_ref, wup_ref, cw_ref, cb_ref, hist_ref, wdn_ref, gpost_ref,
              out_ref, tail_ref, f_ref, buf_ref, carry_ref, acc_ref, a_ref, *, tiles_per_seq, d_ff):
    @pl.when(pl.program_id(0) % tiles_per_seq == 0)
    def _():
        carry_ref[...] = hist_ref[...]

    f_ref[...] = _rms(h_ref[...], gpre_ref[...]).astype(BF16)

    n_chunks = d_ff // COL_CHUNK
    half_cols = lambda c, half: slice(half * d_ff + c * COL_CHUNK, half * d_ff + (c + 1) * COL_CHUNK)
    half_buf = lambda c, half: buf_ref.at[(2 * c + half) % CONV_BUFFERS]

    def up(c):
        for half in range(2):
            cols = half_cols(c, half)
            _conv_stage(_dot(f_ref[...], wup_ref[:, cols]), half_buf(c, half), carry_ref, cols)

    def gate(c):
        g, u = (_conv_apply(cw_ref[:, half_cols(c, half)], half_buf(c, half)) + cb_ref[:, half_cols(c, half)]
                for half in range(2))
        a_ref[c % 2] = (jax.nn.silu(g) * u).astype(BF16)

    def down(c):
        y = _dot(a_ref[c % 2], wdn_ref[c * COL_CHUNK:(c + 1) * COL_CHUNK, :])
        if c == 0:
            acc_ref[...] = y
        else:
            acc_ref[...] += y

    up(0)
    for c in range(n_chunks + 1):
        if c + 1 < n_chunks:
            up(c + 1)
        if c >= 1:
            down(c - 1)
        if c < n_chunks:
            gate(c)

    tail_ref[...] = carry_ref[...]
    out_ref[...] = h_ref[...] + _rms(acc_ref[...], gpost_ref[...])


def _ffn(h, gpre, wup, conv_w, conv_b, hist, wdn, gpost, *, seq_rows):
    rows, d = h.shape
    d_ff = wdn.shape[0]
    tm = min(ROW_TILE, rows)
    assert rows % tm == 0 and seq_rows % tm == 0 and d_ff % COL_CHUNK == 0
    row_spec = pl.BlockSpec((tm, d), lambda i: (i, 0))
    return pl.pallas_call(
        functools.partial(_ffn_body, tiles_per_seq=seq_rows // tm, d_ff=d_ff),
        grid=(rows // tm,),
        in_specs=[row_spec, _resident(gpre.shape), _resident(wup.shape), _resident(conv_w.shape),
                  _resident(conv_b.shape), _resident(hist.shape), _resident(wdn.shape),
                  _resident(gpost.shape)],
        out_specs=[row_spec, pl.BlockSpec((SUBLANES, 2 * d_ff), lambda i: (0, 0))],
        out_shape=[jax.ShapeDtypeStruct((rows, d), F32),
                   jax.ShapeDtypeStruct((SUBLANES, 2 * d_ff), F32)],
        scratch_shapes=[pltpu.VMEM((tm, d), BF16),
                        pltpu.VMEM((CONV_BUFFERS, tm + SUBLANES, COL_CHUNK), F32),
                        pltpu.VMEM((SUBLANES, 2 * d_ff), F32),
                        pltpu.VMEM((tm, d), F32),
                        pltpu.VMEM((2, tm, COL_CHUNK), BF16)],
        compiler_params=pltpu.CompilerParams(dimension_semantics=("arbitrary",),
                                             vmem_limit_bytes=VMEM_LIMIT),
        name="ffn",
    )(h, gpre, wup, conv_w, conv_b, hist, wdn, gpost)


def kernel(x, meta_tokens, w_in, conv_w, w_conv_out, lambda_q1, lambda_k1, lambda_q2, lambda_k2,
           subln_g, w_attn_out, w_mix_out, norm_mix_pre, norm_mix_post, w_ffn_up, ffn_conv_w,
           ffn_conv_b, w_ffn_down, norm_ffn_pre, norm_ffn_post):
    batch, seq, d = x.shape
    depth = w_in.shape[0]
    assert meta_tokens.shape[0] == N_META and seq + N_META <= POS_SPLIT * 256 and seq % ROW_TILE == 0
    slopes = jnp.asarray(np.broadcast_to(_alibi_slopes()[:, None, None], (N_HEADS, 1, LANES)), F32)
    row = lambda a: a.reshape(1, -1)

    hx = x.reshape(batch * seq, d)
    hm = meta_tokens.astype(x.dtype)
    for l in range(depth):
        lam_init = 0.8 - 0.6 * math.exp(-0.3 * l)
        w_in_l = w_in[l].astype(BF16)
        lam_vecs = jnp.stack([lambda_q1[l], lambda_k1[l], lambda_q2[l], lambda_k2[l]]).astype(F32)
        g_sub = row(subln_g[l])
        zero_hist = jnp.zeros((SUBLANES, conv_w.shape[2]), F32)

        ypre_m, q_m, k_m, v_m, sga_m, sgb_m, u_tail = _in_proj(
            hm, row(norm_mix_pre[l]), w_in_l, conv_w[l], zero_hist, seq_rows=N_META, attn_layout=False)
        ypre, qt, ka, vt, sga, sgb, _ = _in_proj(
            hx, row(norm_mix_pre[l]), w_in_l, conv_w[l], u_tail, seq_rows=seq, attn_layout=True)

        o_m = _meta_attention(q_m, k_m, v_m, slopes, lam_vecs, g_sub, lam_init=lam_init)
        o = _attention(qt, ka, vt, k_m, v_m, lam_vecs, g_sub, batch=batch, seq=seq, lam_init=lam_init)

        wc, wa, wm = (w.astype(BF16) for w in (w_conv_out[l], w_attn_out[l], w_mix_out[l]))
        hm = _merge(ypre_m, o_m, sga_m, sgb_m, hm, wc, wa, wm, row(norm_mix_post[l]))
        hx = _merge(ypre, o, sga, sgb, hx, wc, wa, wm, row(norm_mix_post[l]))

        wup, wdn = w_ffn_up[l].astype(BF16), w_ffn_down[l].astype(BF16)
        zero_hist = jnp.zeros((SUBLANES, wup.shape[1]), F32)
        ffn = functools.partial(_ffn, gpre=row(norm_ffn_pre[l]), wup=wup, conv_w=ffn_conv_w[l],
                                conv_b=row(ffn_conv_b[l]), wdn=wdn, gpost=row(norm_ffn_post[l]))
        hm, z_tail = ffn(hm, hist=zero_hist, seq_rows=N_META)
        hx, _ = ffn(hx, hist=z_tail, seq_rows=seq)
    return hx.reshape(batch, seq, d)
```

```python
import functools
import math

import numpy as np
import jax
import jax.numpy as jnp
from jax import lax
from jax.experimental import pallas as pl
from jax.experimental.pallas import tpu as pltpu

F32 = jnp.float32
BF16 = jnp.bfloat16

N_META = 16
N_HEADS = 8
HEAD_DIM = 64
V_DIM = 2 * HEAD_DIM
N_SEG = 8
RMS_EPS = 1e-6
NEG = -1e30

LANES = 128
SUBLANES = 8
ROW_TILE = 512
COL_CHUNK = 256
CONV_BUFFERS = 6
Q_TILE = ROW_TILE
K_TILE = ROW_TILE
ACC_ROWS = V_DIM + 16
HEADS_PER_STEP = 2
VMEM_LIMIT = 56 * 1024 * 1024

POS_SPLIT = 64
COEF_PARTS = 3
LANE_POS_HI = HEAD_DIM
LANE_POS_LO = HEAD_DIM + COEF_PARTS
LOG2E = math.log2(math.e)


def _rms(x, g):
    ms = jnp.mean(x * x, axis=-1, keepdims=True)
    return x * lax.rsqrt(ms + RMS_EPS) * g


def _dot(a, b):
    return jnp.dot(a, b, preferred_element_type=F32)


def _conv_stage(z, buf_ref, carry_ref, cols):
    tm = z.shape[0]
    buf_ref[0:SUBLANES, :] = carry_ref[:, cols]
    buf_ref[SUBLANES:SUBLANES + tm, :] = z
    carry_ref[:, cols] = z[tm - SUBLANES:tm]


def _conv_apply(w, buf_ref):
    tm = buf_ref.shape[0] - SUBLANES
    return (w[2:3] * buf_ref[SUBLANES:SUBLANES + tm, :]
            + w[1:2] * buf_ref[SUBLANES - 1:SUBLANES - 1 + tm, :]
            + w[0:1] * buf_ref[SUBLANES - 2:SUBLANES - 2 + tm, :])


def _causal_conv3(z, w, buf_ref, carry_ref, cols):
    _conv_stage(z, buf_ref, carry_ref, cols)
    return _conv_apply(w, buf_ref)


def _alibi_slopes():
    return 2.0 ** (-8.0 * np.arange(1, N_HEADS + 1, dtype=np.float64) / N_HEADS)


def _bf16_parts(value):
    parts, rest = [], np.float64(value)
    for _ in range(COEF_PARTS):
        part = np.float64(rest.astype(BF16))
        parts.append(float(part))
        rest = rest - part
    return parts


def _lane_select(lane, first_lane, values, otherwise):
    out = otherwise
    for j, value in enumerate(values):
        out = jnp.where(lane == first_lane + j, value, out)
    return out


def _key_features(pos0, n):
    pos = lax.broadcasted_iota(jnp.int32, (n, LANES), 0) + pos0
    lane = lax.broadcasted_iota(jnp.int32, (n, LANES), 1)
    hi = (pos >> 6).astype(F32)
    lo = (pos & (POS_SPLIT - 1)).astype(F32)
    in_hi = (lane >= LANE_POS_HI) & (lane < LANE_POS_HI + COEF_PARTS)
    in_lo = (lane >= LANE_POS_LO) & (lane < LANE_POS_LO + COEF_PARTS)
    return jnp.where(in_hi, hi, jnp.where(in_lo, lo, 0.0)), lane


def _query_features(slope, lane):
    parts = _bf16_parts(slope * LOG2E)
    feat = _lane_select(lane, LANE_POS_HI, [p * POS_SPLIT for p in parts], 0.0)
    return _lane_select(lane, LANE_POS_LO, parts, feat)


def _split_maps(x, feat, lane):
    first = jnp.where(lane < HEAD_DIM, x, feat)
    second = jnp.where(lane < HEAD_DIM, pltpu.roll(x, HEAD_DIM, axis=1), feat)
    return first, second


def _ones_rows(n):
    return jnp.where(lax.broadcasted_iota(jnp.int32, (ACC_ROWS - V_DIM, n), 0) == 0, 1.0, 0.0)


def _in_proj_body(x_ref, g_ref, w_ref, cw_ref, hist_ref,
                  ypre_ref, q_ref, k_ref, v_ref, sga_ref, sgb_ref, tail_ref,
                  xn_ref, buf_ref, carry_ref, *, tiles_per_seq, width, q_scale, attn_layout):
    tile_in_seq = pl.program_id(0) % tiles_per_seq

    @pl.when(tile_in_seq == 0)
    def _():
        carry_ref[...] = hist_ref[...]

    xn_ref[...] = _rms(x_ref[...], g_ref[...]).astype(BF16)
    tm = x_ref.shape[0]
    heads_per_chunk = COL_CHUNK // V_DIM
    if attn_layout:
        slopes = _alibi_slopes()
        kfeat, lane = _key_features(N_META + tile_in_seq * tm, tm)

    for c in range(width // COL_CHUNK):
        cols = slice(c * COL_CHUNK, (c + 1) * COL_CHUNK)

        def proj(seg):
            lo = seg * width + c * COL_CHUNK
            return _dot(xn_ref[...], w_ref[:, lo:lo + COL_CHUNK])

        u = proj(1) * proj(2)
        y = _causal_conv3(u, cw_ref[:, cols], buf_ref.at[c % CONV_BUFFERS], carry_ref, cols)
        ypre_ref[:, cols] = (proj(0) * y).astype(BF16)
        sga_ref[:, cols] = jax.nn.sigmoid(proj(6)).astype(BF16)
        sgb_ref[:, cols] = jax.nn.sigmoid(proj(7)).astype(BF16)
        q, k, v = proj(3) * q_scale, proj(4), proj(5)
        if not attn_layout:
            q_ref[:, cols] = q.astype(BF16)
            k_ref[:, cols] = k.astype(BF16)
            v_ref[:, cols] = v.astype(BF16)
            continue
        for hh in range(heads_per_chunk):
            head = c * heads_per_chunk + hh
            lanes = slice(hh * V_DIM, (hh + 1) * V_DIM)
            qfeat = _query_features(float(slopes[head]), lane)
            for m, qm in enumerate(_split_maps(q[:, lanes], qfeat, lane)):
                q_ref[head, m, 0] = qm.T.astype(BF16)
            for m, km in enumerate(_split_maps(k[:, lanes], kfeat, lane)):
                k_ref[:, (2 * head + m) * LANES:(2 * head + m + 1) * LANES] = km.astype(BF16)
            v_ref[head, 0, 0:V_DIM, :] = v[:, lanes].T.astype(BF16)
            v_ref[head, 0, V_DIM:ACC_ROWS, :] = _ones_rows(tm).astype(BF16)

    tail_ref[...] = carry_ref[...]


def _resident(shape):
    return pl.BlockSpec(shape, lambda i: (0,) * len(shape), pipeline_mode=pl.Buffered(1))


def _in_proj(h, gain, w_in, conv_w, hist, *, seq_rows, attn_layout):
    rows, d = h.shape
    width = w_in.shape[1] // N_SEG
    tm = min(ROW_TILE, rows)
    n_tiles = rows // tm
    assert rows % tm == 0 and seq_rows % tm == 0 and width % COL_CHUNK == 0
    assert width == N_HEADS * V_DIM and COL_CHUNK % V_DIM == 0
    row_spec = lambda n: pl.BlockSpec((tm, n), lambda i: (i, 0))
    act = jax.ShapeDtypeStruct((rows, width), BF16)
    if attn_layout:
        qkv_specs = [pl.BlockSpec((N_HEADS, 2, 1, LANES, tm), lambda i: (0, 0, i, 0, 0)),
                     row_spec(2 * width),
                     pl.BlockSpec((N_HEADS, 1, ACC_ROWS, tm), lambda i: (0, i, 0, 0))]
        qkv_shapes = [jax.ShapeDtypeStruct((N_HEADS, 2, n_tiles, LANES, tm), BF16),
                      jax.ShapeDtypeStruct((rows, 2 * width), BF16),
                      jax.ShapeDtypeStruct((N_HEADS, n_tiles, ACC_ROWS, tm), BF16)]
    else:
        qkv_specs, qkv_shapes = [row_spec(width)] * 3, [act] * 3
    return pl.pallas_call(
        functools.partial(_in_proj_body, tiles_per_seq=seq_rows // tm, width=width,
                          q_scale=HEAD_DIM ** -0.5 * (LOG2E if attn_layout else 1.0),
                          attn_layout=attn_layout),
        grid=(n_tiles,),
        in_specs=[row_spec(d), _resident((1, d)), _resident(w_in.shape), _resident(conv_w.shape),
                  _resident(hist.shape)],
        out_specs=[row_spec(width)] + qkv_specs + [row_spec(width)] * 2
                  + [pl.BlockSpec((SUBLANES, width), lambda i: (0, 0))],
        out_shape=[act] + qkv_shapes + [act] * 2 + [jax.ShapeDtypeStruct((SUBLANES, width), F32)],
        scratch_shapes=[pltpu.VMEM((tm, d), BF16),
                        pltpu.VMEM((CONV_BUFFERS, tm + SUBLANES, COL_CHUNK), F32),
                        pltpu.VMEM((SUBLANES, width), F32)],
        compiler_params=pltpu.CompilerParams(dimension_semantics=("arbitrary",),
                                             vmem_limit_bytes=VMEM_LIMIT),
        name="in_proj",
    )(h, gain, w_in, conv_w, hist)


def _lam(lam_ref, lam_init):
    l = lam_ref[...]
    return (jnp.exp(jnp.sum(l[0:1] * l[1:2], axis=-1, keepdims=True))
            - jnp.exp(jnp.sum(l[2:3] * l[3:4], axis=-1, keepdims=True)) + lam_init)


def _subln(o, g, lam_init):
    return _rms(o, g) * (1.0 - lam_init)


def _attn_body(qt_ref, ka_ref, vt_ref, mk_ref, mv_ref, lam_ref, g_ref, o_ref, mask_ref, *scratch,
               lam_init):
    n_heads, _, n_q, _, tq = qt_ref.shape
    n_k, _, tk = vt_ref.shape[1:]
    ratio = tq // tk
    n_items = ratio * n_q * (n_q + 1) // 2
    streams = [(hd, m) for hd in range(n_heads) for m in range(2)]
    m0_refs, acc0_refs, s_refs, tmax_refs, acc_refs = [], [], [], [], []
    for j in range(len(streams)):
        m0, acc0, s_a, s_b, tm_a, tm_b, acc = scratch[7 * j:7 * j + 7]
        m0_refs.append(m0)
        acc0_refs.append(acc0)
        s_refs.append((s_a, s_b))
        tmax_refs.append((tm_a, tm_b))
        acc_refs.append(acc)
    head_lanes = lambda hd: slice(hd * V_DIM, (hd + 1) * V_DIM)

    @pl.when((pl.program_id(0) == 0) & (pl.program_id(1) == 0))
    def _build_masks():
        kk = lax.broadcasted_iota(jnp.int32, (tk, tq), 0)
        qq = lax.broadcasted_iota(jnp.int32, (tk, tq), 1)
        for d in range(ratio):
            mask_ref[d] = jnp.where(kk + d * tk <= qq, 0.0, NEG)

    feat, lane = _key_features(0, N_META)
    mk_aug, mvt = [], []
    for hd in range(n_heads):
        mk_aug.extend(a.astype(BF16) for a in _split_maps(mk_ref[:, head_lanes(hd)].astype(F32), feat, lane))
        mv_pad = jnp.concatenate([mv_ref[:, head_lanes(hd)].astype(F32),
                                  jnp.zeros((LANES - N_META, LANES), F32)], axis=0)
        mvt.append(jnp.concatenate([mv_pad.T[:, 0:N_META], _ones_rows(N_META)], axis=0).astype(BF16))

    pairs = [(j, i) for j in range(len(streams)) for i in range(n_q)]
    s_meta = [_dot(mk_aug[j], qt_ref[streams[j][0], streams[j][1], i]) for j, i in pairs]
    m0 = [jnp.max(s, axis=0, keepdims=True) for s in s_meta]
    p_meta = [jnp.exp2(s - mx).astype(BF16) for s, mx in zip(s_meta, m0)]
    for (j, i), mx, p in zip(pairs, m0, p_meta):
        m0_refs[j][i] = mx
        acc0_refs[j][i] = _dot(mvt[streams[j][0]], p)

    for j in range(len(streams)):
        for par in range(min(2, n_q)):
            acc_refs[j][par] = acc0_refs[j][par]

    def stage_a(qi, t, slot, masked):
        r0 = pl.multiple_of(t * tk, tk)
        for j, (hd, m) in enumerate(streams):
            keys = ka_ref[pl.ds(r0, tk), (2 * hd + m) * LANES:(2 * hd + m + 1) * LANES]
            s = _dot(keys, qt_ref[hd, m, qi])
            if masked:
                s = s + mask_ref[jnp.maximum(t - qi * ratio, 0)]
            s_refs[j][slot][...] = s
            tmax_refs[j][slot][...] = jnp.max(s, axis=0, keepdims=True)

    def stage_b(qi, t, slot, ms):
        first = t == 0
        par = qi & 1
        out = []
        for j, (hd, m) in enumerate(streams):
            m_old = jnp.where(first, m0_refs[j][qi], ms[j])
            m_new = jnp.maximum(m_old, tmax_refs[j][slot][...])
            p = jnp.exp2(s_refs[j][slot][...] - m_new).astype(BF16)
            acc_refs[j][par] = jnp.exp2(m_old - m_new) * acc_refs[j][par] + _dot(vt_ref[hd, t], p)
            out.append(m_new)
        return tuple(out)

    def finish_query_tile(qi):
        par = qi & 1
        r0 = pl.multiple_of(qi * tq, tq)
        nxt = jnp.minimum(qi + 2, n_q - 1)
        for hd in range(n_heads):
            a1, a2 = acc_refs[2 * hd], acc_refs[2 * hd + 1]
            w1 = 1.0 / a1[par, V_DIM:V_DIM + 1, :]
            w2 = _lam(lam_ref, lam_init) / a2[par, V_DIM:V_DIM + 1, :]
            o_t = a1[par, 0:V_DIM, :] * w1 - a2[par, 0:V_DIM, :] * w2
            scale = lax.rsqrt(jnp.mean(o_t * o_t, axis=0, keepdims=True) + RMS_EPS) * (1.0 - lam_init)
            o_ref[pl.ds(r0, tq), head_lanes(hd)] = ((o_t * scale).T * g_ref[...]).astype(BF16)
            for j in (2 * hd, 2 * hd + 1):
                acc_refs[j][par] = acc0_refs[j][nxt]

    def advance(qi, t):
        end = t + 1 == (qi + 1) * ratio
        more = qi + 1 < n_q
        return (jnp.where(end & more, qi + 1, qi),
                jnp.where(end, jnp.where(more, 0, t), t + 1))

    def iteration(slot_b, carry):
        qa, ta, qb, tb, ms = carry

        def run(masked, finish_previous):
            def f():
                if finish_previous:
                    finish_query_tile(qb - 1)
                stage_a(qa, ta, 1 - slot_b, masked)
                return stage_b(qb, tb, slot_b, ms)
            return f

        masked = ta >= qa * ratio
        ms = lax.cond((tb == 0) & (qb > 0),
                      lambda: lax.cond(masked, run(True, True), run(False, True)),
                      lambda: lax.cond(masked, run(True, False), run(False, False)))
        qn, tn = advance(qa, ta)
        return qn, tn, qa, ta, ms

    zero = jnp.int32(0)
    stage_a(zero, zero, 0, True)
    q1_, t1_ = advance(zero, zero)
    m_init = tuple(jnp.full((1, tq), NEG, F32) for _ in streams)
    carry = lax.fori_loop(0, n_items // 2, lambda i, c: iteration(1, iteration(0, c)),
                          (q1_, t1_, zero, zero, m_init))
    if n_items % 2:
        iteration(0, carry)
    finish_query_tile(jnp.int32(n_q - 1))


def _attention(qt, ka, vt, mk, mv, lam_vecs, g, *, batch, seq, lam_init):
    tq, tk = qt.shape[-1], vt.shape[-1]
    n_q, n_k = seq // tq, seq // tk
    hp = HEADS_PER_STEP
    assert seq % tq == 0 and tq % tk == 0 and tk % LANES == 0 and N_HEADS % hp == 0
    assert qt.shape[2] == batch * n_q and vt.shape[1] == batch * n_k
    full = lambda a: pl.BlockSpec(a.shape, lambda b, h: (0,) * a.ndim)
    meta_spec = pl.BlockSpec((N_META, hp * V_DIM), lambda b, h: (0, h))
    vmem = pltpu.VMEM
    per_stream = [vmem((n_q, 1, tq), F32), vmem((n_q, ACC_ROWS, tq), F32),
                  vmem((tk, tq), F32), vmem((tk, tq), F32), vmem((1, tq), F32), vmem((1, tq), F32),
                  vmem((2, ACC_ROWS, tq), F32)]
    return pl.pallas_call(
        functools.partial(_attn_body, lam_init=lam_init),
        grid=(batch, N_HEADS // hp),
        in_specs=[pl.BlockSpec((hp, 2, n_q, LANES, tq), lambda b, h: (h, 0, b, 0, 0)),
                  pl.BlockSpec((seq, hp * 2 * LANES), lambda b, h: (b, h)),
                  pl.BlockSpec((hp, n_k, ACC_ROWS, tk), lambda b, h: (h, b, 0, 0)),
                  meta_spec, meta_spec, full(lam_vecs), full(g)],
        out_specs=pl.BlockSpec((seq, hp * V_DIM), lambda b, h: (b, h)),
        out_shape=jax.ShapeDtypeStruct((batch * seq, N_HEADS * V_DIM), BF16),
        scratch_shapes=[vmem((tq // tk, tk, tq), F32)] + per_stream * (2 * hp),
        compiler_params=pltpu.CompilerParams(dimension_semantics=("arbitrary", "arbitrary"),
                                             vmem_limit_bytes=VMEM_LIMIT),
        name="diff_attention",
    )(qt, ka, vt, mk, mv, lam_vecs, g)


def _meta_attn_body(q_ref, k_ref, v_ref, slope_ref, lam_ref, g_ref, o_ref, *, lam_init):
    q = q_ref[...]
    k = k_ref[...]
    n = q.shape[0]
    lane = lax.broadcasted_iota(jnp.int32, q.shape, 1)
    zero = jnp.zeros_like(q)
    contract_last = (((1,), (1,)), ((), ()))
    s1 = lax.dot_general(jnp.where(lane < HEAD_DIM, q, zero), k, contract_last,
                         preferred_element_type=F32)
    s2 = lax.dot_general(jnp.where(lane >= HEAD_DIM, q, zero), k, contract_last,
                         preferred_element_type=F32)
    qpos = lax.broadcasted_iota(jnp.int32, (n, n), 0)
    kpos = lax.broadcasted_iota(jnp.int32, (n, n), 1)
    bias = -slope_ref[...][:, 0:1] * (qpos - kpos).astype(F32)

    def softmax(s):
        s = jnp.where(kpos <= qpos, s + bias, NEG)
        e = jnp.exp(s - jnp.max(s, axis=-1, keepdims=True))
        return e / jnp.sum(e, axis=-1, keepdims=True)

    w = softmax(s1) - _lam(lam_ref, lam_init) * softmax(s2)
    o_ref[...] = _subln(_dot(w.astype(BF16), v_ref[...]), g_ref[...], lam_init).astype(BF16)


def _meta_attention(q, k, v, slopes, lam_vecs, g, *, lam_init):
    full = lambda a: pl.BlockSpec(a.shape, lambda h: (0,) * a.ndim)
    head = pl.BlockSpec((N_META, V_DIM), lambda h: (0, h))
    return pl.pallas_call(
        functools.partial(_meta_attn_body, lam_init=lam_init),
        grid=(N_HEADS,),
        in_specs=[head, head, head, pl.BlockSpec((None, 1, LANES), lambda h: (h, 0, 0)),
                  full(lam_vecs), full(g)],
        out_specs=head,
        out_shape=jax.ShapeDtypeStruct(q.shape, BF16),
        name="meta_attention",
    )(q, k, v, slopes, lam_vecs, g)


def _merge_body(ypre_ref, o_ref, sga_ref, sgb_ref, h_ref, wc_ref, wa_ref, wm_ref, g_ref, out_ref,
                gated_ref, acc_ref):
    n_chunks = wm_ref.shape[0] // COL_CHUNK
    chunk = lambda j: slice(j * COL_CHUNK, (j + 1) * COL_CHUNK)

    def branches(j):
        ya = _dot(ypre_ref[...], wc_ref[:, chunk(j)])
        yb = _dot(o_ref[...], wa_ref[:, chunk(j)])
        gated = sga_ref[:, chunk(j)].astype(F32) * ya + sgb_ref[:, chunk(j)].astype(F32) * yb
        gated_ref[j % 2] = gated.astype(BF16)

    def mix(j):
        y = _dot(gated_ref[j % 2], wm_ref[chunk(j), :])
        if j == 0:
            acc_ref[...] = y
        else:
            acc_ref[...] += y

    branches(0)
    for j in range(n_chunks):
        if j + 1 < n_chunks:
            branches(j + 1)
        mix(j)
    out_ref[...] = h_ref[...] + _rms(acc_ref[...], g_ref[...])


def _merge(ypre, o, sga, sgb, h, wc, wa, wm, g):
    rows, d = h.shape
    tm = min(ROW_TILE, rows)
    assert rows % tm == 0
    row_spec = lambda n: pl.BlockSpec((tm, n), lambda i: (i, 0))
    return pl.pallas_call(
        _merge_body,
        grid=(rows // tm,),
        in_specs=[row_spec(ypre.shape[1]), row_spec(o.shape[1]), row_spec(sga.shape[1]),
                  row_spec(sgb.shape[1]), row_spec(d),
                  _resident(wc.shape), _resident(wa.shape), _resident(wm.shape), _resident(g.shape)],
        out_specs=row_spec(d),
        out_shape=jax.ShapeDtypeStruct((rows, d), F32),
        scratch_shapes=[pltpu.VMEM((2, tm, COL_CHUNK), BF16), pltpu.VMEM((tm, d), F32)],
        compiler_params=pltpu.CompilerParams(dimension_semantics=("arbitrary",),
                                             vmem_limit_bytes=VMEM_LIMIT),
        name="merge",
    )(ypre, o, sga, sgb, h, wc, wa, wm, g)


def _ffn_body(h_ref, gpre_ref, wup_ref, cw_ref, cb_ref, hist_ref, wdn_ref, gpost_ref,
              out_ref, tail_ref, f_ref, buf_ref, carry_ref, acc_ref, a_ref, *, tiles_per_seq, d_ff):
    @pl.when(pl.program_id(0) % tiles_per_seq == 0)
    def _():
        carry_ref[...] = hist_ref[...]

    f_ref[...] = _rms(h_ref[...], gpre_ref[...]).astype(BF16)

    n_chunks = d_ff // COL_CHUNK
    half_cols = lambda c, half: slice(half * d_ff + c * COL_CHUNK, half * d_ff + (c + 1) * COL_CHUNK)
    half_buf = lambda c, half: buf_ref.at[(2 * c + half) % CONV_BUFFERS]

    def up(c):
        for half in range(2):
            cols = half_cols(c, half)
            _conv_stage(_dot(f_ref[...], wup_ref[:, cols]), half_buf(c, half), carry_ref, cols)

    def gate(c):
        g, u = (_conv_apply(cw_ref[:, half_cols(c, half)], half_buf(c, half)) + cb_ref[:, half_cols(c, half)]
                for half in range(2))
        a_ref[c % 2] = (jax.nn.silu(g) * u).astype(BF16)

    def down(c):
        y = _dot(a_ref[c % 2], wdn_ref[c * COL_CHUNK:(c + 1) * COL_CHUNK, :])
        if c == 0:
            acc_ref[...] = y
        else:
            acc_ref[...] += y

    assert CONV_BUFFERS >= 6
    up(0)
    up(1)
    for c in range(n_chunks + 1):
        if c + 2 < n_chunks:
            up(c + 2)
        if c >= 1:
            down(c - 1)
        if c < n_chunks:
            gate(c)

    tail_ref[...] = carry_ref[...]
    out_ref[...] = h_ref[...] + _rms(acc_ref[...], gpost_ref[...])


def _ffn(h, gpre, wup, conv_w, conv_b, hist, wdn, gpost, *, seq_rows):
    rows, d = h.shape
    d_ff = wdn.shape[0]
    tm = min(ROW_TILE, rows)
    assert rows % tm == 0 and seq_rows % tm == 0 and d_ff % COL_CHUNK == 0
    row_spec = pl.BlockSpec((tm, d), lambda i: (i, 0))
    return pl.pallas_call(
        functools.partial(_ffn_body, tiles_per_seq=seq_rows // tm, d_ff=d_ff),
        grid=(rows // tm,),
        in_specs=[row_spec, _resident(gpre.shape), _resident(wup.shape), _resident(conv_w.shape),
                  _resident(conv_b.shape), _resident(hist.shape), _resident(wdn.shape),
                  _resident(gpost.shape)],
        out_specs=[row_spec, pl.BlockSpec((SUBLANES, 2 * d_ff), lambda i: (0, 0))],
        out_shape=[jax.ShapeDtypeStruct((rows, d), F32),
                   jax.ShapeDtypeStruct((SUBLANES, 2 * d_ff), F32)],
        scratch_shapes=[pltpu.VMEM((tm, d), BF16),
                        pltpu.VMEM((CONV_BUFFERS, tm + SUBLANES, COL_CHUNK), F32),
                        pltpu.VMEM((SUBLANES, 2 * d_ff), F32),
                        pltpu.VMEM((tm, d), F32),
                        pltpu.VMEM((2, tm, COL_CHUNK), BF16)],
        compiler_params=pltpu.CompilerParams(dimension_semantics=("arbitrary",),
                                             vmem_limit_bytes=VMEM_LIMIT),
        name="ffn",
    )(h, gpre, wup, conv_w, conv_b, hist, wdn, gpost)


def kernel(x, meta_tokens, w_in, conv_w, w_conv_out, lambda_q1, lambda_k1, lambda_q2, lambda_k2,
           subln_g, w_attn_out, w_mix_out, norm_mix_pre, norm_mix_post, w_ffn_up, ffn_conv_w,
           ffn_conv_b, w_ffn_down, norm_ffn_pre, norm_ffn_post):
    batch, seq, d = x.shape
    depth = w_in.shape[0]
    assert meta_tokens.shape[0] == N_META and seq + N_META <= POS_SPLIT * 256 and seq % ROW_TILE == 0
    slopes = jnp.asarray(np.broadcast_to(_alibi_slopes()[:, None, None], (N_HEADS, 1, LANES)), F32)
    row = lambda a: a.reshape(1, -1)

    hx = x.reshape(batch * seq, d)
    hm = meta_tokens.astype(x.dtype)
    for l in range(depth):
        lam_init = 0.8 - 0.6 * math.exp(-0.3 * l)
        w_in_l = w_in[l].astype(BF16)
        lam_vecs = jnp.stack([lambda_q1[l], lambda_k1[l], lambda_q2[l], lambda_k2[l]]).astype(F32)
        g_sub = row(subln_g[l])
        zero_hist = jnp.zeros((SUBLANES, conv_w.shape[2]), F32)

        ypre_m, q_m, k_m, v_m, sga_m, sgb_m, u_tail = _in_proj(
            hm, row(norm_mix_pre[l]), w_in_l, conv_w[l], zero_hist, seq_rows=N_META, attn_layout=False)
        ypre, qt, ka, vt, sga, sgb, _ = _in_proj(
            hx, row(norm_mix_pre[l]), w_in_l, conv_w[l], u_tail, seq_rows=seq, attn_layout=True)

        o_m = _meta_attention(q_m, k_m, v_m, slopes, lam_vecs, g_sub, lam_init=lam_init)
        o = _attention(qt, ka, vt, k_m, v_m, lam_vecs, g_sub, batch=batch, seq=seq, lam_init=lam_init)

        wc, wa, wm = (w.astype(BF16) for w in (w_conv_out[l], w_attn_out[l], w_mix_out[l]))
        hm = _merge(ypre_m, o_m, sga_m, sgb_m, hm, wc, wa, wm, row(norm_mix_post[l]))
        hx = _merge(ypre, o, sga, sgb, hx, wc, wa, wm, row(norm_mix_post[l]))

        wup, wdn = w_ffn_up[l].astype(BF16), w_ffn_down[l].astype(BF16)
        zero_hist = jnp.zeros((SUBLANES, wup.shape[1]), F32)
        ffn = functools.partial(_ffn, gpre=row(norm_ffn_pre[l]), wup=wup, conv_w=ffn_conv_w[l],
                                conv_b=row(ffn_conv_b[l]), wdn=wdn, gpost=row(norm_ffn_post[l]))
        hm, z_tail = ffn(hm, hist=zero_hist, seq_rows=N_META)
        hx, _ = ffn(hx, hist=z_tail, seq_rows=seq)
    return hx.reshape(batch, seq, d)
```

```python
import functools
import math

import numpy as np
import jax
import jax.numpy as jnp
from jax import lax
from jax.experimental import pallas as pl
from jax.experimental.pallas import tpu as pltpu

F32 = jnp.float32
BF16 = jnp.bfloat16

N_META = 16
N_HEADS = 8
HEAD_DIM = 64
V_DIM = 2 * HEAD_DIM
N_SEG = 8
RMS_EPS = 1e-6
NEG = -1e30

LANES = 128
SUBLANES = 8
ROW_TILE = 512
COL_CHUNK = 256
CONV_BUFFERS = 6
Q_TILE = ROW_TILE
K_TILE = ROW_TILE
ACC_ROWS = V_DIM + 16
HEADS_PER_STEP = 2
VMEM_LIMIT = 56 * 1024 * 1024

POS_SPLIT = 64
COEF_PARTS = 3
LANE_POS_HI = HEAD_DIM
LANE_POS_LO = HEAD_DIM + COEF_PARTS
LOG2E = math.log2(math.e)


def _rms(x, g):
    ms = jnp.mean(x * x, axis=-1, keepdims=True)
    return x * lax.rsqrt(ms + RMS_EPS) * g


def _dot(a, b):
    return jnp.dot(a, b, preferred_element_type=F32)


def _conv_stage(z, buf_ref, carry_ref, cols):
    tm = z.shape[0]
    buf_ref[0:SUBLANES, :] = carry_ref[:, cols]
    buf_ref[SUBLANES:SUBLANES + tm, :] = z
    carry_ref[:, cols] = z[tm - SUBLANES:tm]


def _conv_apply(w, buf_ref):
    tm = buf_ref.shape[0] - SUBLANES
    return (w[2:3] * buf_ref[SUBLANES:SUBLANES + tm, :]
            + w[1:2] * buf_ref[SUBLANES - 1:SUBLANES - 1 + tm, :]
            + w[0:1] * buf_ref[SUBLANES - 2:SUBLANES - 2 + tm, :])


def _causal_conv3(z, w, buf_ref, carry_ref, cols):
    _conv_stage(z, buf_ref, carry_ref, cols)
    return _conv_apply(w, buf_ref)


def _alibi_slopes():
    return 2.0 ** (-8.0 * np.arange(1, N_HEADS + 1, dtype=np.float64) / N_HEADS)


def _bf16_parts(value):
    parts, rest = [], np.float64(value)
    for _ in range(COEF_PARTS):
        part = np.float64(rest.astype(BF16))
        parts.append(float(part))
        rest = rest - part
    return parts


def _lane_select(lane, first_lane, values, otherwise):
    out = otherwise
    for j, value in enumerate(values):
        out = jnp.where(lane == first_lane + j, value, out)
    return out


def _key_features(pos0, n):
    pos = lax.broadcasted_iota(jnp.int32, (n, LANES), 0) + pos0
    lane = lax.broadcasted_iota(jnp.int32, (n, LANES), 1)
    hi = (pos >> 6).astype(F32)
    lo = (pos & (POS_SPLIT - 1)).astype(F32)
    in_hi = (lane >= LANE_POS_HI) & (lane < LANE_POS_HI + COEF_PARTS)
    in_lo = (lane >= LANE_POS_LO) & (lane < LANE_POS_LO + COEF_PARTS)
    return jnp.where(in_hi, hi, jnp.where(in_lo, lo, 0.0)), lane


def _query_features(slope, lane):
    parts = _bf16_parts(slope * LOG2E)
    feat = _lane_select(lane, LANE_POS_HI, [p * POS_SPLIT for p in parts], 0.0)
    return _lane_select(lane, LANE_POS_LO, parts, feat)


def _split_maps(x, feat, lane):
    first = jnp.where(lane < HEAD_DIM, x, feat)
    second = jnp.where(lane < HEAD_DIM, pltpu.roll(x, HEAD_DIM, axis=1), feat)
    return first, second


def _ones_rows(n):
    return jnp.where(lax.broadcasted_iota(jnp.int32, (ACC_ROWS - V_DIM, n), 0) == 0, 1.0, 0.0)


def _in_proj_body(x_ref, g_ref, w_ref, cw_ref, hist_ref,
                  ypre_ref, q_ref, k_ref, v_ref, sga_ref, sgb_ref, tail_ref,
                  xn_ref, buf_ref, carry_ref, *, tiles_per_seq, width, q_scale, attn_layout):
    tile_in_seq = pl.program_id(0) % tiles_per_seq

    @pl.when(tile_in_seq == 0)
    def _():
        carry_ref[...] = hist_ref[...]

    xn_ref[...] = _rms(x_ref[...], g_ref[...]).astype(BF16)
    tm = x_ref.shape[0]
    heads_per_chunk = COL_CHUNK // V_DIM
    if attn_layout:
        slopes = _alibi_slopes()
        kfeat, lane = _key_features(N_META + tile_in_seq * tm, tm)

    for c in range(width // COL_CHUNK):
        cols = slice(c * COL_CHUNK, (c + 1) * COL_CHUNK)

        def proj(seg):
            lo = seg * width + c * COL_CHUNK
            return _dot(xn_ref[...], w_ref[:, lo:lo + COL_CHUNK])

        u = proj(1) * proj(2)
        y = _causal_conv3(u, cw_ref[:, cols], buf_ref.at[c % CONV_BUFFERS], carry_ref, cols)
        ypre_ref[:, cols] = (proj(0) * y).astype(BF16)
        sga_ref[:, cols] = jax.nn.sigmoid(proj(6)).astype(BF16)
        sgb_ref[:, cols] = jax.nn.sigmoid(proj(7)).astype(BF16)
        q, k, v = proj(3) * q_scale, proj(4), proj(5)
        if not attn_layout:
            q_ref[:, cols] = q.astype(BF16)
            k_ref[:, cols] = k.astype(BF16)
            v_ref[:, cols] = v.astype(BF16)
            continue
        for hh in range(heads_per_chunk):
            head = c * heads_per_chunk + hh
            lanes = slice(hh * V_DIM, (hh + 1) * V_DIM)
            qfeat = _query_features(float(slopes[head]), lane)
            for m, qm in enumerate(_split_maps(q[:, lanes], qfeat, lane)):
                q_ref[head, m, 0] = qm.T.astype(BF16)
            for m, km in enumerate(_split_maps(k[:, lanes], kfeat, lane)):
                k_ref[:, (2 * head + m) * LANES:(2 * head + m + 1) * LANES] = km.astype(BF16)
            v_ref[head, 0, 0:V_DIM, :] = v[:, lanes].T.astype(BF16)
            v_ref[head, 0, V_DIM:ACC_ROWS, :] = _ones_rows(tm).astype(BF16)

    tail_ref[...] = carry_ref[...]


def _resident(shape):
    return pl.BlockSpec(shape, lambda i: (0,) * len(shape), pipeline_mode=pl.Buffered(1))


def _in_proj(h, gain, w_in, conv_w, hist, *, seq_rows, attn_layout):
    rows, d = h.shape
    width = w_in.shape[1] // N_SEG
    tm = min(ROW_TILE, rows)
    n_tiles = rows // tm
    assert rows % tm == 0 and seq_rows % tm == 0 and width % COL_CHUNK == 0
    assert width == N_HEADS * V_DIM and COL_CHUNK % V_DIM == 0
    row_spec = lambda n: pl.BlockSpec((tm, n), lambda i: (i, 0))
    act = jax.ShapeDtypeStruct((rows, width), BF16)
    if attn_layout:
        qkv_specs = [pl.BlockSpec((N_HEADS, 2, 1, LANES, tm), lambda i: (0, 0, i, 0, 0)),
                     row_spec(2 * width),
                     pl.BlockSpec((N_HEADS, 1, ACC_ROWS, tm), lambda i: (0, i, 0, 0))]
        qkv_shapes = [jax.ShapeDtypeStruct((N_HEADS, 2, n_tiles, LANES, tm), BF16),
                      jax.ShapeDtypeStruct((rows, 2 * width), BF16),
                      jax.ShapeDtypeStruct((N_HEADS, n_tiles, ACC_ROWS, tm), BF16)]
    else:
        qkv_specs, qkv_shapes = [row_spec(width)] * 3, [act] * 3
    return pl.pallas_call(
        functools.partial(_in_proj_body, tiles_per_seq=seq_rows // tm, width=width,
                          q_scale=HEAD_DIM ** -0.5 * (LOG2E if attn_layout else 1.0),
                          attn_layout=attn_layout),
        grid=(n_tiles,),
        in_specs=[row_spec(d), _resident((1, d)), _resident(w_in.shape), _resident(conv_w.shape),
                  _resident(hist.shape)],
        out_specs=[row_spec(width)] + qkv_specs + [row_spec(width)] * 2
                  + [pl.BlockSpec((SUBLANES, width), lambda i: (0, 0))],
        out_shape=[act] + qkv_shapes + [act] * 2 + [jax.ShapeDtypeStruct((SUBLANES, width), F32)],
        scratch_shapes=[pltpu.VMEM((tm, d), BF16),
                        pltpu.VMEM((CONV_BUFFERS, tm + SUBLANES, COL_CHUNK), F32),
                        pltpu.VMEM((SUBLANES, width), F32)],
        compiler_params=pltpu.CompilerParams(dimension_semantics=("arbitrary",),
                                             vmem_limit_bytes=VMEM_LIMIT),
        name="in_proj",
    )(h, gain, w_in, conv_w, hist)


def _lam(lam_ref, lam_init):
    l = lam_ref[...]
    return (jnp.exp(jnp.sum(l[0:1] * l[1:2], axis=-1, keepdims=True))
            - jnp.exp(jnp.sum(l[2:3] * l[3:4], axis=-1, keepdims=True)) + lam_init)


def _subln(o, g, lam_init):
    return _rms(o, g) * (1.0 - lam_init)


def _attn_body(qt_ref, ka_ref, vt_ref, mk_ref, mv_ref, lam_ref, g_ref, o_ref, mask_ref, *scratch,
               lam_init):
    n_heads, _, n_q, _, tq = qt_ref.shape
    n_k, _, tk = vt_ref.shape[1:]
    ratio = tq // tk
    n_items = ratio * n_q * (n_q + 1) // 2
    streams = [(hd, m) for hd in range(n_heads) for m in range(2)]
    m0_refs, acc0_refs, s_refs, tmax_refs, acc_refs = [], [], [], [], []
    for j in range(len(streams)):
        m0, acc0, s_a, s_b, tm_a, tm_b, acc = scratch[7 * j:7 * j + 7]
        m0_refs.append(m0)
        acc0_refs.append(acc0)
        s_refs.append((s_a, s_b))
        tmax_refs.append((tm_a, tm_b))
        acc_refs.append(acc)
    head_lanes = lambda hd: slice(hd * V_DIM, (hd + 1) * V_DIM)

    @pl.when((pl.program_id(0) == 0) & (pl.program_id(1) == 0))
    def _build_masks():
        kk = lax.broadcasted_iota(jnp.int32, (tk, tq), 0)
        qq = lax.broadcasted_iota(jnp.int32, (tk, tq), 1)
        for d in range(ratio):
            mask_ref[d] = jnp.where(kk + d * tk <= qq, 0.0, NEG)

    def stage_a(j, qi, t, slot, masked):
        hd, m = streams[j]
        r0 = pl.multiple_of(t * tk, tk)
        keys = ka_ref[pl.ds(r0, tk), (2 * hd + m) * LANES:(2 * hd + m + 1) * LANES]
        s = _dot(keys, qt_ref[hd, m, qi])
        if masked:
            s = s + mask_ref[jnp.maximum(t - qi * ratio, 0)]
        s_refs[j][slot][...] = s
        tmax_refs[j][slot][...] = jnp.max(s, axis=0, keepdims=True)

    feat, lane = _key_features(0, N_META)
    mk_aug, mvt = [], []
    for hd in range(n_heads):
        mk_aug.extend(a.astype(BF16) for a in _split_maps(mk_ref[:, head_lanes(hd)].astype(F32), feat, lane))
        mv_pad = jnp.concatenate([mv_ref[:, head_lanes(hd)].astype(F32),
                                  jnp.zeros((LANES - N_META, LANES), F32)], axis=0)
        mvt.append(jnp.concatenate([mv_pad.T[:, 0:N_META], _ones_rows(N_META)], axis=0).astype(BF16))

    pairs = [(j, i) for j in range(len(streams)) for i in range(n_q)]
    s_meta = [_dot(mk_aug[j], qt_ref[streams[j][0], streams[j][1], i]) for j, i in pairs]
    m0 = [jnp.max(s, axis=0, keepdims=True) for s in s_meta]
    p_meta = [jnp.exp2(s - mx).astype(BF16) for s, mx in zip(s_meta, m0)]
    for (j, i), mx, p in zip(pairs, m0, p_meta):
        m0_refs[j][i] = mx
        acc0_refs[j][i] = _dot(mvt[streams[j][0]], p)

    for j in range(len(streams)):
        for par in range(min(2, n_q)):
            acc_refs[j][par] = acc0_refs[j][par]

    def stage_b(j, qi, t, slot, m_prev):
        par = qi & 1
        m_old = jnp.where(t == 0, m0_refs[j][qi], m_prev)
        m_new = jnp.maximum(m_old, tmax_refs[j][slot][...])
        p = jnp.exp2(s_refs[j][slot][...] - m_new).astype(BF16)
        acc_refs[j][par] = (jnp.exp2(m_old - m_new) * acc_refs[j][par]
                            + _dot(vt_ref[streams[j][0], t], p))
        return m_new

    def stages(qa, ta, slot_a, masked, qb, tb, slot_b, ms):
        out = []
        for j in range(len(streams)):
            stage_a(j, qa, ta, slot_a, masked)
            out.append(stage_b(j, qb, tb, slot_b, ms[j]))
        return tuple(out)

    def finish_query_tile(qi):
        par = qi & 1
        r0 = pl.multiple_of(qi * tq, tq)
        nxt = jnp.minimum(qi + 2, n_q - 1)
        for hd in range(n_heads):
            a1, a2 = acc_refs[2 * hd], acc_refs[2 * hd + 1]
            w1 = 1.0 / a1[par, V_DIM:V_DIM + 1, :]
            w2 = _lam(lam_ref, lam_init) / a2[par, V_DIM:V_DIM + 1, :]
            o_t = a1[par, 0:V_DIM, :] * w1 - a2[par, 0:V_DIM, :] * w2
            scale = lax.rsqrt(jnp.mean(o_t * o_t, axis=0, keepdims=True) + RMS_EPS) * (1.0 - lam_init)
            o_ref[pl.ds(r0, tq), head_lanes(hd)] = ((o_t * scale).T * g_ref[...]).astype(BF16)
            for j in (2 * hd, 2 * hd + 1):
                acc_refs[j][par] = acc0_refs[j][nxt]

    def advance(qi, t):
        end = t + 1 == (qi + 1) * ratio
        more = qi + 1 < n_q
        return (jnp.where(end & more, qi + 1, qi),
                jnp.where(end, jnp.where(more, 0, t), t + 1))

    def iteration(slot_b, carry):
        qa, ta, qb, tb, ms = carry

        def run(masked, finish_previous):
            def f():
                if finish_previous:
                    finish_query_tile(qb - 1)
                return stages(qa, ta, 1 - slot_b, masked, qb, tb, slot_b, ms)
            return f

        masked = ta >= qa * ratio
        ms = lax.cond((tb == 0) & (qb > 0),
                      lambda: lax.cond(masked, run(True, True), run(False, True)),
                      lambda: lax.cond(masked, run(True, False), run(False, False)))
        qn, tn = advance(qa, ta)
        return qn, tn, qa, ta, ms

    zero = jnp.int32(0)
    for j in range(len(streams)):
        stage_a(j, zero, zero, 0, True)
    q1_, t1_ = advance(zero, zero)
    m_init = tuple(jnp.full((1, tq), NEG, F32) for _ in streams)
    carry = lax.fori_loop(0, n_items // 2, lambda i, c: iteration(1, iteration(0, c)),
                          (q1_, t1_, zero, zero, m_init))
    if n_items % 2:
        iteration(0, carry)
    finish_query_tile(jnp.int32(n_q - 1))


def _attention(qt, ka, vt, mk, mv, lam_vecs, g, *, batch, seq, lam_init):
    tq, tk = qt.shape[-1], vt.shape[-1]
    n_q, n_k = seq // tq, seq // tk
    hp = HEADS_PER_STEP
    assert seq % tq == 0 and tq % tk == 0 and tk % LANES == 0 and N_HEADS % hp == 0
    assert qt.shape[2] == batch * n_q and vt.shape[1] == batch * n_k
    full = lambda a: pl.BlockSpec(a.shape, lambda b, h: (0,) * a.ndim)
    meta_spec = pl.BlockSpec((N_META, hp * V_DIM), lambda b, h: (0, h))
    vmem = pltpu.VMEM
    per_stream = [vmem((n_q, 1, tq), F32), vmem((n_q, ACC_ROWS, tq), F32),
                  vmem((tk, tq), F32), vmem((tk, tq), F32), vmem((1, tq), F32), vmem((1, tq), F32),
                  vmem((2, ACC_ROWS, tq), F32)]
    return pl.pallas_call(
        functools.partial(_attn_body, lam_init=lam_init),
        grid=(batch, N_HEADS // hp),
        in_specs=[pl.BlockSpec((hp, 2, n_q, LANES, tq), lambda b, h: (h, 0, b, 0, 0)),
                  pl.BlockSpec((seq, hp * 2 * LANES), lambda b, h: (b, h)),
                  pl.BlockSpec((hp, n_k, ACC_ROWS, tk), lambda b, h: (h, b, 0, 0)),
                  meta_spec, meta_spec, full(lam_vecs), full(g)],
        out_specs=pl.BlockSpec((seq, hp * V_DIM), lambda b, h: (b, h)),
        out_shape=jax.ShapeDtypeStruct((batch * seq, N_HEADS * V_DIM), BF16),
        scratch_shapes=[vmem((tq // tk, tk, tq), F32)] + per_stream * (2 * hp),
        compiler_params=pltpu.CompilerParams(dimension_semantics=("arbitrary", "arbitrary"),
                                             vmem_limit_bytes=VMEM_LIMIT),
        name="diff_attention",
    )(qt, ka, vt, mk, mv, lam_vecs, g)


def _meta_attn_body(q_ref, k_ref, v_ref, slope_ref, lam_ref, g_ref, o_ref, *, lam_init):
    q = q_ref[...]
    k = k_ref[...]
    n = q.shape[0]
    lane = lax.broadcasted_iota(jnp.int32, q.shape, 1)
    zero = jnp.zeros_like(q)
    contract_last = (((1,), (1,)), ((), ()))
    s1 = lax.dot_general(jnp.where(lane < HEAD_DIM, q, zero), k, contract_last,
                         preferred_element_type=F32)
    s2 = lax.dot_general(jnp.where(lane >= HEAD_DIM, q, zero), k, contract_last,
                         preferred_element_type=F32)
    qpos = lax.broadcasted_iota(jnp.int32, (n, n), 0)
    kpos = lax.broadcasted_iota(jnp.int32, (n, n), 1)
    bias = -slope_ref[...][:, 0:1] * (qpos - kpos).astype(F32)

    def softmax(s):
        s = jnp.where(kpos <= qpos, s + bias, NEG)
        e = jnp.exp(s - jnp.max(s, axis=-1, keepdims=True))
        return e / jnp.sum(e, axis=-1, keepdims=True)

    w = softmax(s1) - _lam(lam_ref, lam_init) * softmax(s2)
    o_ref[...] = _subln(_dot(w.astype(BF16), v_ref[...]), g_ref[...], lam_init).astype(BF16)


def _meta_attention(q, k, v, slopes, lam_vecs, g, *, lam_init):
    full = lambda a: pl.BlockSpec(a.shape, lambda h: (0,) * a.ndim)
    head = pl.BlockSpec((N_META, V_DIM), lambda h: (0, h))
    return pl.pallas_call(
        functools.partial(_meta_attn_body, lam_init=lam_init),
        grid=(N_HEADS,),
        in_specs=[head, head, head, pl.BlockSpec((None, 1, LANES), lambda h: (h, 0, 0)),
                  full(lam_vecs), full(g)],
        out_specs=head,
        out_shape=jax.ShapeDtypeStruct(q.shape, BF16),
        name="meta_attention",
    )(q, k, v, slopes, lam_vecs, g)


def _merge_body(ypre_ref, o_ref, sga_ref, sgb_ref, h_ref, wc_ref, wa_ref, wm_ref, g_ref, out_ref,
                gated_ref, acc_ref):
    n_chunks = wm_ref.shape[0] // COL_CHUNK
    chunk = lambda j: slice(j * COL_CHUNK, (j + 1) * COL_CHUNK)

    def branches(j):
        ya = _dot(ypre_ref[...], wc_ref[:, chunk(j)])
        yb = _dot(o_ref[...], wa_ref[:, chunk(j)])
        gated = sga_ref[:, chunk(j)].astype(F32) * ya + sgb_ref[:, chunk(j)].astype(F32) * yb
        gated_ref[j % 2] = gated.astype(BF16)

    def mix(j):
        y = _dot(gated_ref[j % 2], wm_ref[chunk(j), :])
        if j == 0:
            acc_ref[...] = y
        else:
            acc_ref[...] += y

    branches(0)
    for j in range(n_chunks):
        if j + 1 < n_chunks:
            branches(j + 1)
        mix(j)
    out_ref[...] = h_ref[...] + _rms(acc_ref[...], g_ref[...])


def _merge(ypre, o, sga, sgb, h, wc, wa, wm, g):
    rows, d = h.shape
    tm = min(ROW_TILE, rows)
    assert rows % tm == 0
    row_spec = lambda n: pl.BlockSpec((tm, n), lambda i: (i, 0))
    return pl.pallas_call(
        _merge_body,
        grid=(rows // tm,),
        in_specs=[row_spec(ypre.shape[1]), row_spec(o.shape[1]), row_spec(sga.shape[1]),
                  row_spec(sgb.shape[1]), row_spec(d),
                  _resident(wc.shape), _resident(wa.shape), _resident(wm.shape), _resident(g.shape)],
        out_specs=row_spec(d),
        out_shape=jax.ShapeDtypeStruct((rows, d), F32),
        scratch_shapes=[pltpu.VMEM((2, tm, COL_CHUNK), BF16), pltpu.VMEM((tm, d), F32)],
        compiler_params=pltpu.CompilerParams(dimension_semantics=("arbitrary",),
                                             vmem_limit_bytes=VMEM_LIMIT),
        name="merge",
    )(ypre, o, sga, sgb, h, wc, wa, wm, g)


def _ffn_body(h_ref, gpre_ref, wup_ref, cw_ref, cb_ref, hist_ref, wdn_ref, gpost_ref,
              out_ref, tail_ref, f_ref, buf_ref, carry_ref, acc_ref, a_ref, *, tiles_per_seq, d_ff):
    @pl.when(pl.program_id(0) % tiles_per_seq == 0)
    def _():
        carry_ref[...] = hist_ref[...]

    f_ref[...] = _rms(h_ref[...], gpre_ref[...]).astype(BF16)

    n_chunks = d_ff // COL_CHUNK
    half_cols = lambda c, half: slice(half * d_ff + c * COL_CHUNK, half * d_ff + (c + 1) * COL_CHUNK)
    half_buf = lambda c, half: buf_ref.at[(2 * c + half) % CONV_BUFFERS]

    def up(c):
        for half in range(2):
            cols = half_cols(c, half)
            _conv_stage(_dot(f_ref[...], wup_ref[:, cols]), half_buf(c, half), carry_ref, cols)

    def gate(c):
        g, u = (_conv_apply(cw_ref[:, half_cols(c, half)], half_buf(c, half)) + cb_ref[:, half_cols(c, half)]
                for half in range(2))
        a_ref[c % 2] = (jax.nn.silu(g) * u).astype(BF16)

    def down(c):
        y = _dot(a_ref[c % 2], wdn_ref[c * COL_CHUNK:(c + 1) * COL_CHUNK, :])
        if c == 0:
            acc_ref[...] = y
        else:
            acc_ref[...] += y

    assert CONV_BUFFERS >= 6
    up(0)
    up(1)
    for c in range(n_chunks + 1):
        if c + 2 < n_chunks:
            up(c + 2)
        if c >= 1:
            down(c - 1)
        if c < n_chunks:
            gate(c)

    tail_ref[...] = carry_ref[...]
    out_ref[...] = h_ref[...] + _rms(acc_ref[...], gpost_ref[...])


def _ffn(h, gpre, wup, conv_w, conv_b, hist, wdn, gpost, *, seq_rows):
    rows, d = h.shape
    d_ff = wdn.shape[0]
    tm = min(ROW_TILE, rows)
    assert rows % tm == 0 and seq_rows % tm == 0 and d_ff % COL_CHUNK == 0
    row_spec = pl.BlockSpec((tm, d), lambda i: (i, 0))
    return pl.pallas_call(
        functools.partial(_ffn_body, tiles_per_seq=seq_rows // tm, d_ff=d_ff),
        grid=(rows // tm,),
        in_specs=[row_spec, _resident(gpre.shape), _resident(wup.shape), _resident(conv_w.shape),
                  _resident(conv_b.shape), _resident(hist.shape), _resident(wdn.shape),
                  _resident(gpost.shape)],
        out_specs=[row_spec, pl.BlockSpec((SUBLANES, 2 * d_ff), lambda i: (0, 0))],
        out_shape=[jax.ShapeDtypeStruct((rows, d), F32),
                   jax.ShapeDtypeStruct((SUBLANES, 2 * d_ff), F32)],
        scratch_shapes=[pltpu.VMEM((tm, d), BF16),
                        pltpu.VMEM((CONV_BUFFERS, tm + SUBLANES, COL_CHUNK), F32),
                        pltpu.VMEM((SUBLANES, 2 * d_ff), F32),
                        pltpu.VMEM((tm, d), F32),
                        pltpu.VMEM((2, tm, COL_CHUNK), BF16)],
        compiler_params=pltpu.CompilerParams(dimension_semantics=("arbitrary",),
                                             vmem_limit_bytes=VMEM_LIMIT),
        name="ffn",
    )(h, gpre, wup, conv_w, conv_b, hist, wdn, gpost)


def kernel(x, meta_tokens, w_in, conv_w, w_conv_out, lambda_q1, lambda_k1, lambda_q2, lambda_k2,
           subln_g, w_attn_out, w_mix_out, norm_mix_pre, norm_mix_post, w_ffn_up, ffn_conv_w,
           ffn_conv_b, w_ffn_down, norm_ffn_pre, norm_ffn_post):
    batch, seq, d = x.shape
    depth = w_in.shape[0]
    assert meta_tokens.shape[0] == N_META and seq + N_META <= POS_SPLIT * 256 and seq % ROW_TILE == 0
    slopes = jnp.asarray(np.broadcast_to(_alibi_slopes()[:, None, None], (N_HEADS, 1, LANES)), F32)
    row = lambda a: a.reshape(1, -1)

    hx = x.reshape(batch * seq, d)
    hm = meta_tokens.astype(x.dtype)
    for l in range(depth):
        lam_init = 0.8 - 0.6 * math.exp(-0.3 * l)
        w_in_l = w_in[l].astype(BF16)
        lam_vecs = jnp.stack([lambda_q1[l], lambda_k1[l], lambda_q2[l], lambda_k2[l]]).astype(F32)
        g_sub = row(subln_g[l])
        zero_hist = jnp.zeros((SUBLANES, conv_w.shape[2]), F32)

        ypre_m, q_m, k_m, v_m, sga_m, sgb_m, u_tail = _in_proj(
            hm, row(norm_mix_pre[l]), w_in_l, conv_w[l], zero_hist, seq_rows=N_META, attn_layout=False)
        ypre, qt, ka, vt, sga, sgb, _ = _in_proj(
            hx, row(norm_mix_pre[l]), w_in_l, conv_w[l], u_tail, seq_rows=seq, attn_layout=True)

        o_m = _meta_attention(q_m, k_m, v_m, slopes, lam_vecs, g_sub, lam_init=lam_init)
        o = _attention(qt, ka, vt, k_m, v_m, lam_vecs, g_sub, batch=batch, seq=seq, lam_init=lam_init)

        wc, wa, wm = (w.astype(BF16) for w in (w_conv_out[l], w_attn_out[l], w_mix_out[l]))
        hm = _merge(ypre_m, o_m, sga_m, sgb_m, hm, wc, wa, wm, row(norm_mix_post[l]))
        hx = _merge(ypre, o, sga, sgb, hx, wc, wa, wm, row(norm_mix_post[l]))

        wup, wdn = w_ffn_up[l].astype(BF16), w_ffn_down[l].astype(BF16)
        zero_hist = jnp.zeros((SUBLANES, wup.shape[1]), F32)
        ffn = functools.partial(_ffn, gpre=row(norm_ffn_pre[l]), wup=wup, conv_w=ffn_conv_w[l],
                                conv_b=row(ffn_conv_b[l]), wdn=wdn, gpost=row(norm_ffn_post[l]))
        hm, z_tail = ffn(hm, hist=zero_hist, seq_rows=N_META)
        hx, _ = ffn(hx, hist=z_tail, seq_rows=seq)
    return hx.reshape(batch, seq, d)
```

```python
import functools
import math

import numpy as np
import jax
import jax.numpy as jnp
from jax import lax
from jax.experimental import pallas as pl
from jax.experimental.pallas import tpu as pltpu

F32 = jnp.float32
BF16 = jnp.bfloat16

N_META = 16
N_HEADS = 8
HEAD_DIM = 64
V_DIM = 2 * HEAD_DIM
N_SEG = 8
RMS_EPS = 1e-6
NEG = -1e30

LANES = 128
SUBLANES = 8
ROW_TILE = 512
COL_CHUNK = 256
CONV_BUFFERS = 6
DOWN_GROUP = 2
Q_TILE = ROW_TILE
K_TILE = ROW_TILE
ACC_ROWS = V_DIM + 16
HEADS_PER_STEP = 2
VMEM_LIMIT = 56 * 1024 * 1024

POS_SPLIT = 64
COEF_PARTS = 3
LANE_POS_HI = HEAD_DIM
LANE_POS_LO = HEAD_DIM + COEF_PARTS
LOG2E = math.log2(math.e)


def _rms(x, g):
    ms = jnp.mean(x * x, axis=-1, keepdims=True)
    return x * lax.rsqrt(ms + RMS_EPS) * g


def _dot(a, b):
    return jnp.dot(a, b, preferred_element_type=F32)


def _conv_stage(z, buf_ref, carry_ref, cols):
    tm = z.shape[0]
    buf_ref[0:SUBLANES, :] = carry_ref[:, cols]
    buf_ref[SUBLANES:SUBLANES + tm, :] = z
    carry_ref[:, cols] = z[tm - SUBLANES:tm]


def _conv_apply(w, buf_ref):
    tm = buf_ref.shape[0] - SUBLANES
    return (w[2:3] * buf_ref[SUBLANES:SUBLANES + tm, :]
            + w[1:2] * buf_ref[SUBLANES - 1:SUBLANES - 1 + tm, :]
            + w[0:1] * buf_ref[SUBLANES - 2:SUBLANES - 2 + tm, :])


def _causal_conv3(z, w, buf_ref, carry_ref, cols):
    _conv_stage(z, buf_ref, carry_ref, cols)
    return _conv_apply(w, buf_ref)


def _alibi_slopes():
    return 2.0 ** (-8.0 * np.arange(1, N_HEADS + 1, dtype=np.float64) / N_HEADS)


def _bf16_parts(value):
    parts, rest = [], np.float64(value)
    for _ in range(COEF_PARTS):
        part = np.float64(rest.astype(BF16))
        parts.append(float(part))
        rest = rest - part
    return parts


def _lane_select(lane, first_lane, values, otherwise):
    out = otherwise
    for j, value in enumerate(values):
        out = jnp.where(lane == first_lane + j, value, out)
    return out


def _key_features(pos0, n):
    pos = lax.broadcasted_iota(jnp.int32, (n, LANES), 0) + pos0
    lane = lax.broadcasted_iota(jnp.int32, (n, LANES), 1)
    hi = (pos >> 6).astype(F32)
    lo = (pos & (POS_SPLIT - 1)).astype(F32)
    in_hi = (lane >= LANE_POS_HI) & (lane < LANE_POS_HI + COEF_PARTS)
    in_lo = (lane >= LANE_POS_LO) & (lane < LANE_POS_LO + COEF_PARTS)
    return jnp.where(in_hi, hi, jnp.where(in_lo, lo, 0.0)), lane


def _query_features(slope, lane):
    parts = _bf16_parts(slope * LOG2E)
    feat = _lane_select(lane, LANE_POS_HI, [p * POS_SPLIT for p in parts], 0.0)
    return _lane_select(lane, LANE_POS_LO, parts, feat)


def _split_maps(x, feat, lane):
    first = jnp.where(lane < HEAD_DIM, x, feat)
    second = jnp.where(lane < HEAD_DIM, pltpu.roll(x, HEAD_DIM, axis=1), feat)
    return first, second


def _ones_rows(n):
    return jnp.where(lax.broadcasted_iota(jnp.int32, (ACC_ROWS - V_DIM, n), 0) == 0, 1.0, 0.0)


def _in_proj_body(x_ref, g_ref, w_ref, cw_ref, hist_ref,
                  ypre_ref, q_ref, k_ref, v_ref, sga_ref, sgb_ref, tail_ref,
                  xn_ref, buf_ref, carry_ref, *, tiles_per_seq, width, q_scale, attn_layout):
    tile_in_seq = pl.program_id(0) % tiles_per_seq

    @pl.when(tile_in_seq == 0)
    def _():
        carry_ref[...] = hist_ref[...]

    xn_ref[...] = _rms(x_ref[...], g_ref[...]).astype(BF16)
    tm = x_ref.shape[0]
    heads_per_chunk = COL_CHUNK // V_DIM
    if attn_layout:
        slopes = _alibi_slopes()
        kfeat, lane = _key_features(N_META + tile_in_seq * tm, tm)

    for c in range(width // COL_CHUNK):
        cols = slice(c * COL_CHUNK, (c + 1) * COL_CHUNK)

        def proj(seg):
            lo = seg * width + c * COL_CHUNK
            return _dot(xn_ref[...], w_ref[:, lo:lo + COL_CHUNK])

        u = proj(1) * proj(2)
        y = _causal_conv3(u, cw_ref[:, cols], buf_ref.at[c % CONV_BUFFERS], carry_ref, cols)
        ypre_ref[:, cols] = (proj(0) * y).astype(BF16)
        sga_ref[:, cols] = jax.nn.sigmoid(proj(6)).astype(BF16)
        sgb_ref[:, cols] = jax.nn.sigmoid(proj(7)).astype(BF16)
        q, k, v = proj(3) * q_scale, proj(4), proj(5)
        if not attn_layout:
            q_ref[:, cols] = q.astype(BF16)
            k_ref[:, cols] = k.astype(BF16)
            v_ref[:, cols] = v.astype(BF16)
            continue
        for hh in range(heads_per_chunk):
            head = c * heads_per_chunk + hh
            lanes = slice(hh * V_DIM, (hh + 1) * V_DIM)
            qfeat = _query_features(float(slopes[head]), lane)
            for m, qm in enumerate(_split_maps(q[:, lanes], qfeat, lane)):
                q_ref[head, m, 0] = qm.T.astype(BF16)
            for m, km in enumerate(_split_maps(k[:, lanes], kfeat, lane)):
                k_ref[:, (2 * head + m) * LANES:(2 * head + m + 1) * LANES] = km.astype(BF16)
            v_ref[head, 0, 0:V_DIM, :] = v[:, lanes].T.astype(BF16)
            v_ref[head, 0, V_DIM:ACC_ROWS, :] = _ones_rows(tm).astype(BF16)

    tail_ref[...] = carry_ref[...]


def _resident(shape):
    return pl.BlockSpec(shape, lambda i: (0,) * len(shape), pipeline_mode=pl.Buffered(1))


def _in_proj(h, gain, w_in, conv_w, hist, *, seq_rows, attn_layout):
    rows, d = h.shape
    width = w_in.shape[1] // N_SEG
    tm = min(ROW_TILE, rows)
    n_tiles = rows // tm
    assert rows % tm == 0 and seq_rows % tm == 0 and width % COL_CHUNK == 0
    assert width == N_HEADS * V_DIM and COL_CHUNK % V_DIM == 0
    row_spec = lambda n: pl.BlockSpec((tm, n), lambda i: (i, 0))
    act = jax.ShapeDtypeStruct((rows, width), BF16)
    if attn_layout:
        qkv_specs = [pl.BlockSpec((N_HEADS, 2, 1, LANES, tm), lambda i: (0, 0, i, 0, 0)),
                     row_spec(2 * width),
                     pl.BlockSpec((N_HEADS, 1, ACC_ROWS, tm), lambda i: (0, i, 0, 0))]
        qkv_shapes = [jax.ShapeDtypeStruct((N_HEADS, 2, n_tiles, LANES, tm), BF16),
                      jax.ShapeDtypeStruct((rows, 2 * width), BF16),
                      jax.ShapeDtypeStruct((N_HEADS, n_tiles, ACC_ROWS, tm), BF16)]
    else:
        qkv_specs, qkv_shapes = [row_spec(width)] * 3, [act] * 3
    return pl.pallas_call(
        functools.partial(_in_proj_body, tiles_per_seq=seq_rows // tm, width=width,
                          q_scale=HEAD_DIM ** -0.5 * (LOG2E if attn_layout else 1.0),
                          attn_layout=attn_layout),
        grid=(n_tiles,),
        in_specs=[row_spec(d), _resident((1, d)), _resident(w_in.shape), _resident(conv_w.shape),
                  _resident(hist.shape)],
        out_specs=[row_spec(width)] + qkv_specs + [row_spec(width)] * 2
                  + [pl.BlockSpec((SUBLANES, width), lambda i: (0, 0))],
        out_shape=[act] + qkv_shapes + [act] * 2 + [jax.ShapeDtypeStruct((SUBLANES, width), F32)],
        scratch_shapes=[pltpu.VMEM((tm, d), BF16),
                        pltpu.VMEM((CONV_BUFFERS, tm + SUBLANES, COL_CHUNK), F32),
                        pltpu.VMEM((SUBLANES, width), F32)],
        compiler_params=pltpu.CompilerParams(dimension_semantics=("arbitrary",),
                                             vmem_limit_bytes=VMEM_LIMIT),
        name="in_proj",
    )(h, gain, w_in, conv_w, hist)


def _lam(lam_ref, lam_init):
    l = lam_ref[...]
    return (jnp.exp(jnp.sum(l[0:1] * l[1:2], axis=-1, keepdims=True))
            - jnp.exp(jnp.sum(l[2:3] * l[3:4], axis=-1, keepdims=True)) + lam_init)


def _subln(o, g, lam_init):
    return _rms(o, g) * (1.0 - lam_init)


def _attn_body(qt_ref, ka_ref, vt_ref, mk_ref, mv_ref, lam_ref, g_ref, o_ref, mask_ref, *scratch,
               lam_init):
    n_heads, _, n_q, _, tq = qt_ref.shape
    n_k, _, tk = vt_ref.shape[1:]
    ratio = tq // tk
    n_items = ratio * n_q * (n_q + 1) // 2
    streams = [(hd, m) for hd in range(n_heads) for m in range(2)]
    m0_refs, acc0_refs, s_refs, tmax_refs, acc_refs = [], [], [], [], []
    for j in range(len(streams)):
        m0, acc0, s_a, s_b, tm_a, tm_b, acc = scratch[7 * j:7 * j + 7]
        m0_refs.append(m0)
        acc0_refs.append(acc0)
        s_refs.append((s_a, s_b))
        tmax_refs.append((tm_a, tm_b))
        acc_refs.append(acc)
    head_lanes = lambda hd: slice(hd * V_DIM, (hd + 1) * V_DIM)

    @pl.when((pl.program_id(0) == 0) & (pl.program_id(1) == 0))
    def _build_masks():
        kk = lax.broadcasted_iota(jnp.int32, (tk, tq), 0)
        qq = lax.broadcasted_iota(jnp.int32, (tk, tq), 1)
        for d in range(ratio):
            mask_ref[d] = jnp.where(kk + d * tk <= qq, 0.0, NEG)

    def stage_a(j, qi, t, slot, masked):
        hd, m = streams[j]
        r0 = pl.multiple_of(t * tk, tk)
        keys = ka_ref[pl.ds(r0, tk), (2 * hd + m) * LANES:(2 * hd + m + 1) * LANES]
        s = _dot(keys, qt_ref[hd, m, qi])
        if masked:
            s = s + mask_ref[jnp.maximum(t - qi * ratio, 0)]
        s_refs[j][slot][...] = s
        tmax_refs[j][slot][...] = jnp.max(s, axis=0, keepdims=True)

    feat, lane = _key_features(0, N_META)
    mk_aug, mvt = [], []
    for hd in range(n_heads):
        mk_aug.extend(a.astype(BF16) for a in _split_maps(mk_ref[:, head_lanes(hd)].astype(F32), feat, lane))
        mv_pad = jnp.concatenate([mv_ref[:, head_lanes(hd)].astype(F32),
                                  jnp.zeros((LANES - N_META, LANES), F32)], axis=0)
        mvt.append(jnp.concatenate([mv_pad.T[:, 0:N_META], _ones_rows(N_META)], axis=0).astype(BF16))

    pairs = [(j, i) for j in range(len(streams)) for i in range(n_q)]
    s_meta = [_dot(mk_aug[j], qt_ref[streams[j][0], streams[j][1], i]) for j, i in pairs]
    m0 = [jnp.max(s, axis=0, keepdims=True) for s in s_meta]
    p_meta = [jnp.exp2(s - mx).astype(BF16) for s, mx in zip(s_meta, m0)]
    for (j, i), mx, p in zip(pairs, m0, p_meta):
        m0_refs[j][i] = mx
        acc0_refs[j][i] = _dot(mvt[streams[j][0]], p)

    for j in range(len(streams)):
        for par in range(min(2, n_q)):
            acc_refs[j][par] = acc0_refs[j][par]

    def stage_b(j, qi, t, slot, m_prev):
        par = qi & 1
        m_old = jnp.where(t == 0, m0_refs[j][qi], m_prev)
        m_new = jnp.maximum(m_old, tmax_refs[j][slot][...])
        p = jnp.exp2(s_refs[j][slot][...] - m_new).astype(BF16)
        acc_refs[j][par] = (jnp.exp2(m_old - m_new) * acc_refs[j][par]
                            + _dot(vt_ref[streams[j][0], t], p))
        return m_new

    def stages(qa, ta, slot_a, masked, qb, tb, slot_b, ms):
        out = []
        for j in range(len(streams)):
            stage_a(j, qa, ta, slot_a, masked)
            out.append(stage_b(j, qb, tb, slot_b, ms[j]))
        return tuple(out)

    def finish_query_tile(qi):
        par = qi & 1
        r0 = pl.multiple_of(qi * tq, tq)
        nxt = jnp.minimum(qi + 2, n_q - 1)
        for hd in range(n_heads):
            a1, a2 = acc_refs[2 * hd], acc_refs[2 * hd + 1]
            w1 = 1.0 / a1[par, V_DIM:V_DIM + 1, :]
            w2 = _lam(lam_ref, lam_init) / a2[par, V_DIM:V_DIM + 1, :]
            o_t = a1[par, 0:V_DIM, :] * w1 - a2[par, 0:V_DIM, :] * w2
            scale = lax.rsqrt(jnp.mean(o_t * o_t, axis=0, keepdims=True) + RMS_EPS) * (1.0 - lam_init)
            o_ref[pl.ds(r0, tq), head_lanes(hd)] = ((o_t * scale).T * g_ref[...]).astype(BF16)
            for j in (2 * hd, 2 * hd + 1):
                acc_refs[j][par] = acc0_refs[j][nxt]

    def advance(qi, t):
        end = t + 1 == (qi + 1) * ratio
        more = qi + 1 < n_q
        return (jnp.where(end & more, qi + 1, qi),
                jnp.where(end, jnp.where(more, 0, t), t + 1))

    def iteration(slot_b, carry):
        qa, ta, qb, tb, ms = carry

        def run(masked, finish_previous):
            def f():
                if finish_previous:
                    finish_query_tile(qb - 1)
                return stages(qa, ta, 1 - slot_b, masked, qb, tb, slot_b, ms)
            return f

        masked = ta >= qa * ratio
        ms = lax.cond((tb == 0) & (qb > 0),
                      lambda: lax.cond(masked, run(True, True), run(False, True)),
                      lambda: lax.cond(masked, run(True, False), run(False, False)))
        qn, tn = advance(qa, ta)
        return qn, tn, qa, ta, ms

    zero = jnp.int32(0)
    for j in range(len(streams)):
        stage_a(j, zero, zero, 0, True)
    q1_, t1_ = advance(zero, zero)
    m_init = tuple(jnp.full((1, tq), NEG, F32) for _ in streams)
    carry = lax.fori_loop(0, n_items // 2, lambda i, c: iteration(1, iteration(0, c)),
                          (q1_, t1_, zero, zero, m_init))
    if n_items % 2:
        iteration(0, carry)
    finish_query_tile(jnp.int32(n_q - 1))


def _attention(qt, ka, vt, mk, mv, lam_vecs, g, *, batch, seq, lam_init):
    tq, tk = qt.shape[-1], vt.shape[-1]
    n_q, n_k = seq // tq, seq // tk
    hp = HEADS_PER_STEP
    assert seq % tq == 0 and tq % tk == 0 and tk % LANES == 0 and N_HEADS % hp == 0
    assert qt.shape[2] == batch * n_q and vt.shape[1] == batch * n_k
    full = lambda a: pl.BlockSpec(a.shape, lambda b, h: (0,) * a.ndim)
    meta_spec = pl.BlockSpec((N_META, hp * V_DIM), lambda b, h: (0, h))
    vmem = pltpu.VMEM
    per_stream = [vmem((n_q, 1, tq), F32), vmem((n_q, ACC_ROWS, tq), F32),
                  vmem((tk, tq), F32), vmem((tk, tq), F32), vmem((1, tq), F32), vmem((1, tq), F32),
                  vmem((2, ACC_ROWS, tq), F32)]
    return pl.pallas_call(
        functools.partial(_attn_body, lam_init=lam_init),
        grid=(batch, N_HEADS // hp),
        in_specs=[pl.BlockSpec((hp, 2, n_q, LANES, tq), lambda b, h: (h, 0, b, 0, 0)),
                  pl.BlockSpec((seq, hp * 2 * LANES), lambda b, h: (b, h)),
                  pl.BlockSpec((hp, n_k, ACC_ROWS, tk), lambda b, h: (h, b, 0, 0)),
                  meta_spec, meta_spec, full(lam_vecs), full(g)],
        out_specs=pl.BlockSpec((seq, hp * V_DIM), lambda b, h: (b, h)),
        out_shape=jax.ShapeDtypeStruct((batch * seq, N_HEADS * V_DIM), BF16),
        scratch_shapes=[vmem((tq // tk, tk, tq), F32)] + per_stream * (2 * hp),
        compiler_params=pltpu.CompilerParams(dimension_semantics=("arbitrary", "arbitrary"),
                                             vmem_limit_bytes=VMEM_LIMIT),
        name="diff_attention",
    )(qt, ka, vt, mk, mv, lam_vecs, g)


def _meta_attn_body(q_ref, k_ref, v_ref, slope_ref, lam_ref, g_ref, o_ref, *, lam_init):
    q = q_ref[...]
    k = k_ref[...]
    n = q.shape[0]
    lane = lax.broadcasted_iota(jnp.int32, q.shape, 1)
    zero = jnp.zeros_like(q)
    contract_last = (((1,), (1,)), ((), ()))
    s1 = lax.dot_general(jnp.where(lane < HEAD_DIM, q, zero), k, contract_last,
                         preferred_element_type=F32)
    s2 = lax.dot_general(jnp.where(lane >= HEAD_DIM, q, zero), k, contract_last,
                         preferred_element_type=F32)
    qpos = lax.broadcasted_iota(jnp.int32, (n, n), 0)
    kpos = lax.broadcasted_iota(jnp.int32, (n, n), 1)
    bias = -slope_ref[...][:, 0:1] * (qpos - kpos).astype(F32)

    def softmax(s):
        s = jnp.where(kpos <= qpos, s + bias, NEG)
        e = jnp.exp(s - jnp.max(s, axis=-1, keepdims=True))
        return e / jnp.sum(e, axis=-1, keepdims=True)

    w = softmax(s1) - _lam(lam_ref, lam_init) * softmax(s2)
    o_ref[...] = _subln(_dot(w.astype(BF16), v_ref[...]), g_ref[...], lam_init).astype(BF16)


def _meta_attention(q, k, v, slopes, lam_vecs, g, *, lam_init):
    full = lambda a: pl.BlockSpec(a.shape, lambda h: (0,) * a.ndim)
    head = pl.BlockSpec((N_META, V_DIM), lambda h: (0, h))
    return pl.pallas_call(
        functools.partial(_meta_attn_body, lam_init=lam_init),
        grid=(N_HEADS,),
        in_specs=[head, head, head, pl.BlockSpec((None, 1, LANES), lambda h: (h, 0, 0)),
                  full(lam_vecs), full(g)],
        out_specs=head,
        out_shape=jax.ShapeDtypeStruct(q.shape, BF16),
        name="meta_attention",
    )(q, k, v, slopes, lam_vecs, g)


def _merge_body(ypre_ref, o_ref, sga_ref, sgb_ref, h_ref, wc_ref, wa_ref, wm_ref, g_ref, out_ref,
                gated_ref, acc_ref):
    n_chunks = wm_ref.shape[0] // COL_CHUNK
    chunk = lambda j: slice(j * COL_CHUNK, (j + 1) * COL_CHUNK)

    def branches(j):
        ya = _dot(ypre_ref[...], wc_ref[:, chunk(j)])
        yb = _dot(o_ref[...], wa_ref[:, chunk(j)])
        gated = sga_ref[:, chunk(j)].astype(F32) * ya + sgb_ref[:, chunk(j)].astype(F32) * yb
        gated_ref[j % 2] = gated.astype(BF16)

    def mix(j):
        y = _dot(gated_ref[j % 2], wm_ref[chunk(j), :])
        if j == 0:
            acc_ref[...] = y
        else:
            acc_ref[...] += y

    branches(0)
    for j in range(n_chunks):
        if j + 1 < n_chunks:
            branches(j + 1)
        mix(j)
    out_ref[...] = h_ref[...] + _rms(acc_ref[...], g_ref[...])


def _merge(ypre, o, sga, sgb, h, wc, wa, wm, g):
    rows, d = h.shape
    tm = min(ROW_TILE, rows)
    assert rows % tm == 0
    row_spec = lambda n: pl.BlockSpec((tm, n), lambda i: (i, 0))
    return pl.pallas_call(
        _merge_body,
        grid=(rows // tm,),
        in_specs=[row_spec(ypre.shape[1]), row_spec(o.shape[1]), row_spec(sga.shape[1]),
                  row_spec(sgb.shape[1]), row_spec(d),
                  _resident(wc.shape), _resident(wa.shape), _resident(wm.shape), _resident(g.shape)],
        out_specs=row_spec(d),
        out_shape=jax.ShapeDtypeStruct((rows, d), F32),
        scratch_shapes=[pltpu.VMEM((2, tm, COL_CHUNK), BF16), pltpu.VMEM((tm, d), F32)],
        compiler_params=pltpu.CompilerParams(dimension_semantics=("arbitrary",),
                                             vmem_limit_bytes=VMEM_LIMIT),
        name="merge",
    )(ypre, o, sga, sgb, h, wc, wa, wm, g)


def _ffn_body(h_ref, gpre_ref, wup_ref, cw_ref, cb_ref, hist_ref, wdn_ref, gpost_ref,
              out_ref, tail_ref, f_ref, buf_ref, carry_ref, acc_ref, a_ref, *, tiles_per_seq, d_ff):
    @pl.when(pl.program_id(0) % tiles_per_seq == 0)
    def _():
        carry_ref[...] = hist_ref[...]

    f_ref[...] = _rms(h_ref[...], gpre_ref[...]).astype(BF16)

    n_chunks = d_ff // COL_CHUNK
    half_cols = lambda c, half: slice(half * d_ff + c * COL_CHUNK, half * d_ff + (c + 1) * COL_CHUNK)
    half_buf = lambda c, half: buf_ref.at[(2 * c + half) % CONV_BUFFERS]

    def up(c):
        for half in range(2):
            cols = half_cols(c, half)
            _conv_stage(_dot(f_ref[...], wup_ref[:, cols]), half_buf(c, half), carry_ref, cols)

    def gate(c):
        g, u = (_conv_apply(cw_ref[:, half_cols(c, half)], half_buf(c, half)) + cb_ref[:, half_cols(c, half)]
                for half in range(2))
        a_ref[c % (2 * DOWN_GROUP)] = (jax.nn.silu(g) * u).astype(BF16)

    def down(c0, n):
        a = jnp.concatenate([a_ref[(c0 + i) % (2 * DOWN_GROUP)] for i in range(n)], axis=1)
        y = _dot(a, wdn_ref[c0 * COL_CHUNK:(c0 + n) * COL_CHUNK, :])
        if c0 == 0:
            acc_ref[...] = y
        else:
            acc_ref[...] += y

    assert CONV_BUFFERS >= 6
    up(0)
    up(1)
    for c in range(n_chunks + 1):
        if c + 2 < n_chunks:
            up(c + 2)
        if c >= DOWN_GROUP and c % DOWN_GROUP == 0:
            down(c - DOWN_GROUP, DOWN_GROUP)
        if c < n_chunks:
            gate(c)
    if n_chunks % DOWN_GROUP:
        down(n_chunks - n_chunks % DOWN_GROUP, n_chunks % DOWN_GROUP)

    tail_ref[...] = carry_ref[...]
    out_ref[...] = h_ref[...] + _rms(acc_ref[...], gpost_ref[...])


def _ffn(h, gpre, wup, conv_w, conv_b, hist, wdn, gpost, *, seq_rows):
    rows, d = h.shape
    d_ff = wdn.shape[0]
    tm = min(ROW_TILE, rows)
    assert rows % tm == 0 and seq_rows % tm == 0 and d_ff % COL_CHUNK == 0
    row_spec = pl.BlockSpec((tm, d), lambda i: (i, 0))
    return pl.pallas_call(
        functools.partial(_ffn_body, tiles_per_seq=seq_rows // tm, d_ff=d_ff),
        grid=(rows // tm,),
        in_specs=[row_spec, _resident(gpre.shape), _resident(wup.shape), _resident(conv_w.shape),
                  _resident(conv_b.shape), _resident(hist.shape), _resident(wdn.shape),
                  _resident(gpost.shape)],
        out_specs=[row_spec, pl.BlockSpec((SUBLANES, 2 * d_ff), lambda i: (0, 0))],
        out_shape=[jax.ShapeDtypeStruct((rows, d), F32),
                   jax.ShapeDtypeStruct((SUBLANES, 2 * d_ff), F32)],
        scratch_shapes=[pltpu.VMEM((tm, d), BF16),
                        pltpu.VMEM((CONV_BUFFERS, tm + SUBLANES, COL_CHUNK), F32),
                        pltpu.VMEM((SUBLANES, 2 * d_ff), F32),
                        pltpu.VMEM((tm, d), F32),
                        pltpu.VMEM((2 * DOWN_GROUP, tm, COL_CHUNK), BF16)],
        compiler_params=pltpu.CompilerParams(dimension_semantics=("arbitrary",),
                                             vmem_limit_bytes=VMEM_LIMIT),
        name="ffn",
    )(h, gpre, wup, conv_w, conv_b, hist, wdn, gpost)


def kernel(x, meta_tokens, w_in, conv_w, w_conv_out, lambda_q1, lambda_k1, lambda_q2, lambda_k2,
           subln_g, w_attn_out, w_mix_out, norm_mix_pre, norm_mix_post, w_ffn_up, ffn_conv_w,
           ffn_conv_b, w_ffn_down, norm_ffn_pre, norm_ffn_post):
    batch, seq, d = x.shape
    depth = w_in.shape[0]
    assert meta_tokens.shape[0] == N_META and seq + N_META <= POS_SPLIT * 256 and seq % ROW_TILE == 0
    slopes = jnp.asarray(np.broadcast_to(_alibi_slopes()[:, None, None], (N_HEADS, 1, LANES)), F32)
    row = lambda a: a.reshape(1, -1)

    hx = x.reshape(batch * seq, d)
    hm = meta_tokens.astype(x.dtype)
    for l in range(depth):
        lam_init = 0.8 - 0.6 * math.exp(-0.3 * l)
        w_in_l = w_in[l].astype(BF16)
        lam_vecs = jnp.stack([lambda_q1[l], lambda_k1[l], lambda_q2[l], lambda_k2[l]]).astype(F32)
        g_sub = row(subln_g[l])
        zero_hist = jnp.zeros((SUBLANES, conv_w.shape[2]), F32)

        ypre_m, q_m, k_m, v_m, sga_m, sgb_m, u_tail = _in_proj(
            hm, row(norm_mix_pre[l]), w_in_l, conv_w[l], zero_hist, seq_rows=N_META, attn_layout=False)
        ypre, qt, ka, vt, sga, sgb, _ = _in_proj(
            hx, row(norm_mix_pre[l]), w_in_l, conv_w[l], u_tail, seq_rows=seq, attn_layout=True)

        o_m = _meta_attention(q_m, k_m, v_m, slopes, lam_vecs, g_sub, lam_init=lam_init)
        o = _attention(qt, ka, vt, k_m, v_m, lam_vecs, g_sub, batch=batch, seq=seq, lam_init=lam_init)

        wc, wa, wm = (w.astype(BF16) for w in (w_conv_out[l], w_attn_out[l], w_mix_out[l]))
        hm = _merge(ypre_m, o_m, sga_m, sgb_m, hm, wc, wa, wm, row(norm_mix_post[l]))
        hx = _merge(ypre, o, sga, sgb, hx, wc, wa, wm, row(norm_mix_post[l]))

        wup, wdn = w_ffn_up[l].astype(BF16), w_ffn_down[l].astype(BF16)
        zero_hist = jnp.zeros((SUBLANES, wup.shape[1]), F32)
        ffn = functools.partial(_ffn, gpre=row(norm_ffn_pre[l]), wup=wup, conv_w=ffn_conv_w[l],
                                conv_b=row(ffn_conv_b[l]), wdn=wdn, gpost=row(norm_ffn_post[l]))
        hm, z_tail = ffn(hm, hist=zero_hist, seq_rows=N_META)
        hx, _ = ffn(hx, hist=z_tail, seq_rows=seq)
    return hx.reshape(batch, seq, d)
```

```python
import functools
import math

import numpy as np
import jax
import jax.numpy as jnp
from jax import lax
from jax.experimental import pallas as pl
from jax.experimental.pallas import tpu as pltpu

F32 = jnp.float32
BF16 = jnp.bfloat16

N_META = 16
N_HEADS = 8
HEAD_DIM = 64
V_DIM = 2 * HEAD_DIM
N_SEG = 8
RMS_EPS = 1e-6
NEG = -1e30

LANES = 128
SUBLANES = 8
ROW_TILE = 512
COL_CHUNK = 256
CONV_BUFFERS = 6
DOWN_GROUP = 2
Q_TILE = ROW_TILE
K_TILE = ROW_TILE
ACC_ROWS = V_DIM + 16
HEADS_PER_STEP = 2
VMEM_LIMIT = 56 * 1024 * 1024

POS_SPLIT = 64
COEF_PARTS = 3
LANE_POS_HI = HEAD_DIM
LANE_POS_LO = HEAD_DIM + COEF_PARTS
LOG2E = math.log2(math.e)


def _rms(x, g):
    ms = jnp.mean(x * x, axis=-1, keepdims=True)
    return x * lax.rsqrt(ms + RMS_EPS) * g


def _dot(a, b):
    return jnp.dot(a, b, preferred_element_type=F32)


def _conv_stage(z, buf_ref, carry_ref, cols):
    tm = z.shape[0]
    buf_ref[0:SUBLANES, :] = carry_ref[:, cols]
    buf_ref[SUBLANES:SUBLANES + tm, :] = z
    carry_ref[:, cols] = z[tm - SUBLANES:tm]


def _conv_apply(w, buf_ref):
    tm = buf_ref.shape[0] - SUBLANES
    return (w[2:3] * buf_ref[SUBLANES:SUBLANES + tm, :]
            + w[1:2] * buf_ref[SUBLANES - 1:SUBLANES - 1 + tm, :]
            + w[0:1] * buf_ref[SUBLANES - 2:SUBLANES - 2 + tm, :])


def _causal_conv3(z, w, buf_ref, carry_ref, cols):
    _conv_stage(z, buf_ref, carry_ref, cols)
    return _conv_apply(w, buf_ref)


def _alibi_slopes():
    return 2.0 ** (-8.0 * np.arange(1, N_HEADS + 1, dtype=np.float64) / N_HEADS)


def _bf16_parts(value):
    parts, rest = [], np.float64(value)
    for _ in range(COEF_PARTS):
        part = np.float64(rest.astype(BF16))
        parts.append(float(part))
        rest = rest - part
    return parts


def _lane_select(lane, first_lane, values, otherwise):
    out = otherwise
    for j, value in enumerate(values):
        out = jnp.where(lane == first_lane + j, value, out)
    return out


def _key_features(pos0, n):
    pos = lax.broadcasted_iota(jnp.int32, (n, LANES), 0) + pos0
    lane = lax.broadcasted_iota(jnp.int32, (n, LANES), 1)
    hi = (pos >> 6).astype(F32)
    lo = (pos & (POS_SPLIT - 1)).astype(F32)
    in_hi = (lane >= LANE_POS_HI) & (lane < LANE_POS_HI + COEF_PARTS)
    in_lo = (lane >= LANE_POS_LO) & (lane < LANE_POS_LO + COEF_PARTS)
    return jnp.where(in_hi, hi, jnp.where(in_lo, lo, 0.0)), lane


def _query_features(slope, lane):
    parts = _bf16_parts(slope * LOG2E)
    feat = _lane_select(lane, LANE_POS_HI, [p * POS_SPLIT for p in parts], 0.0)
    return _lane_select(lane, LANE_POS_LO, parts, feat)


def _split_maps(x, feat, lane):
    first = jnp.where(lane < HEAD_DIM, x, feat)
    second = jnp.where(lane < HEAD_DIM, pltpu.roll(x, HEAD_DIM, axis=1), feat)
    return first, second


def _ones_rows(n):
    return jnp.where(lax.broadcasted_iota(jnp.int32, (ACC_ROWS - V_DIM, n), 0) == 0, 1.0, 0.0)


def _in_proj_body(x_ref, g_ref, w_ref, cw_ref, hist_ref,
                  ypre_ref, q_ref, k_ref, v_ref, sga_ref, sgb_ref, tail_ref,
                  xn_ref, buf_ref, carry_ref, *, tiles_per_seq, width, q_scale, attn_layout):
    tile_in_seq = pl.program_id(0) % tiles_per_seq

    @pl.when(tile_in_seq == 0)
    def _():
        carry_ref[...] = hist_ref[...]

    xn_ref[...] = _rms(x_ref[...], g_ref[...]).astype(BF16)
    tm = x_ref.shape[0]
    heads_per_chunk = COL_CHUNK // V_DIM
    if attn_layout:
        slopes = _alibi_slopes()
        kfeat, lane = _key_features(N_META + tile_in_seq * tm, tm)

    for c in range(width // COL_CHUNK):
        cols = slice(c * COL_CHUNK, (c + 1) * COL_CHUNK)

        def proj(seg):
            lo = seg * width + c * COL_CHUNK
            return _dot(xn_ref[...], w_ref[:, lo:lo + COL_CHUNK])

        u = proj(1) * proj(2)
        y = _causal_conv3(u, cw_ref[:, cols], buf_ref.at[c % CONV_BUFFERS], carry_ref, cols)
        ypre_ref[:, cols] = (proj(0) * y).astype(BF16)
        sga_ref[:, cols] = jax.nn.sigmoid(proj(6)).astype(BF16)
        sgb_ref[:, cols] = jax.nn.sigmoid(proj(7)).astype(BF16)
        q, k, v = proj(3) * q_scale, proj(4), proj(5)
        if not attn_layout:
            q_ref[:, cols] = q.astype(BF16)
            k_ref[:, cols] = k.astype(BF16)
            v_ref[:, cols] = v.astype(BF16)
            continue
        for hh in range(heads_per_chunk):
            head = c * heads_per_chunk + hh
            lanes = slice(hh * V_DIM, (hh + 1) * V_DIM)
            qfeat = _query_features(float(slopes[head]), lane)
            for m, qm in enumerate(_split_maps(q[:, lanes], qfeat, lane)):
                q_ref[head, m, 0] = qm.T.astype(BF16)
            for m, km in enumerate(_split_maps(k[:, lanes], kfeat, lane)):
                k_ref[:, (2 * head + m) * LANES:(2 * head + m + 1) * LANES] = km.astype(BF16)
            v_ref[head, 0, 0:V_DIM, :] = v[:, lanes].T.astype(BF16)
            v_ref[head, 0, V_DIM:ACC_ROWS, :] = _ones_rows(tm).astype(BF16)

    tail_ref[...] = carry_ref[...]


def _resident(shape):
    return pl.BlockSpec(shape, lambda i: (0,) * len(shape), pipeline_mode=pl.Buffered(1))


def _in_proj(h, gain, w_in, conv_w, hist, *, seq_rows, attn_layout):
    rows, d = h.shape
    width = w_in.shape[1] // N_SEG
    tm = min(ROW_TILE, rows)
    n_tiles = rows // tm
    assert rows % tm == 0 and seq_rows % tm == 0 and width % COL_CHUNK == 0
    assert width == N_HEADS * V_DIM and COL_CHUNK % V_DIM == 0
    row_spec = lambda n: pl.BlockSpec((tm, n), lambda i: (i, 0))
    act = jax.ShapeDtypeStruct((rows, width), BF16)
    if attn_layout:
        qkv_specs = [pl.BlockSpec((N_HEADS, 2, 1, LANES, tm), lambda i: (0, 0, i, 0, 0)),
                     row_spec(2 * width),
                     pl.BlockSpec((N_HEADS, 1, ACC_ROWS, tm), lambda i: (0, i, 0, 0))]
        qkv_shapes = [jax.ShapeDtypeStruct((N_HEADS, 2, n_tiles, LANES, tm), BF16),
                      jax.ShapeDtypeStruct((rows, 2 * width), BF16),
                      jax.ShapeDtypeStruct((N_HEADS, n_tiles, ACC_ROWS, tm), BF16)]
    else:
        qkv_specs, qkv_shapes = [row_spec(width)] * 3, [act] * 3
    return pl.pallas_call(
        functools.partial(_in_proj_body, tiles_per_seq=seq_rows // tm, width=width,
                          q_scale=HEAD_DIM ** -0.5 * (LOG2E if attn_layout else 1.0),
                          attn_layout=attn_layout),
        grid=(n_tiles,),
        in_specs=[row_spec(d), _resident((1, d)), _resident(w_in.shape), _resident(conv_w.shape),
                  _resident(hist.shape)],
        out_specs=[row_spec(width)] + qkv_specs + [row_spec(width)] * 2
                  + [pl.BlockSpec((SUBLANES, width), lambda i: (0, 0))],
        out_shape=[act] + qkv_shapes + [act] * 2 + [jax.ShapeDtypeStruct((SUBLANES, width), F32)],
        scratch_shapes=[pltpu.VMEM((tm, d), BF16),
                        pltpu.VMEM((CONV_BUFFERS, tm + SUBLANES, COL_CHUNK), F32),
                        pltpu.VMEM((SUBLANES, width), F32)],
        compiler_params=pltpu.CompilerParams(dimension_semantics=("arbitrary",),
                                             vmem_limit_bytes=VMEM_LIMIT),
        name="in_proj",
    )(h, gain, w_in, conv_w, hist)


def _lam(lam_ref, lam_init):
    l = lam_ref[...]
    return (jnp.exp(jnp.sum(l[0:1] * l[1:2], axis=-1, keepdims=True))
            - jnp.exp(jnp.sum(l[2:3] * l[3:4], axis=-1, keepdims=True)) + lam_init)


def _subln(o, g, lam_init):
    return _rms(o, g) * (1.0 - lam_init)


def _attn_body(qt_ref, ka_ref, vt_ref, mk_ref, mv_ref, lam_ref, g_ref, o_ref, mask_ref, *scratch,
               lam_init):
    n_heads, _, n_q, _, tq = qt_ref.shape
    n_k, _, tk = vt_ref.shape[1:]
    ratio = tq // tk
    n_items = ratio * n_q * (n_q + 1) // 2
    streams = [(hd, m) for hd in range(n_heads) for m in range(2)]
    m0_refs, acc0_refs, s_refs, tmax_refs, acc_refs = [], [], [], [], []
    for j in range(len(streams)):
        m0, acc0, s_a, s_b, tm_a, tm_b, acc = scratch[7 * j:7 * j + 7]
        m0_refs.append(m0)
        acc0_refs.append(acc0)
        s_refs.append((s_a, s_b))
        tmax_refs.append((tm_a, tm_b))
        acc_refs.append(acc)
    head_lanes = lambda hd: slice(hd * V_DIM, (hd + 1) * V_DIM)

    @pl.when((pl.program_id(0) == 0) & (pl.program_id(1) == 0))
    def _build_masks():
        kk = lax.broadcasted_iota(jnp.int32, (tk, tq), 0)
        qq = lax.broadcasted_iota(jnp.int32, (tk, tq), 1)
        for d in range(ratio):
            mask_ref[d] = jnp.where(kk + d * tk <= qq, 0.0, NEG)

    def stage_a(j, qi, t, slot, masked):
        hd, m = streams[j]
        r0 = pl.multiple_of(t * tk, tk)
        keys = ka_ref[pl.ds(r0, tk), (2 * hd + m) * LANES:(2 * hd + m + 1) * LANES]
        s = _dot(keys, qt_ref[hd, m, qi])
        if masked:
            s = s + mask_ref[jnp.maximum(t - qi * ratio, 0)]
        s_refs[j][slot][...] = s
        tmax_refs[j][slot][...] = jnp.max(s, axis=0, keepdims=True)

    feat, lane = _key_features(0, N_META)
    mk_aug, mvt = [], []
    for hd in range(n_heads):
        mk_aug.extend(a.astype(BF16) for a in _split_maps(mk_ref[:, head_lanes(hd)].astype(F32), feat, lane))
        mv_pad = jnp.concatenate([mv_ref[:, head_lanes(hd)].astype(F32),
                                  jnp.zeros((LANES - N_META, LANES), F32)], axis=0)
        mvt.append(jnp.concatenate([mv_pad.T[:, 0:N_META], _ones_rows(N_META)], axis=0).astype(BF16))

    pairs = [(j, i) for j in range(len(streams)) for i in range(n_q)]
    s_meta = [_dot(mk_aug[j], qt_ref[streams[j][0], streams[j][1], i]) for j, i in pairs]
    m0 = [jnp.max(s, axis=0, keepdims=True) for s in s_meta]
    p_meta = [jnp.exp2(s - mx).astype(BF16) for s, mx in zip(s_meta, m0)]
    for (j, i), mx, p in zip(pairs, m0, p_meta):
        m0_refs[j][i] = mx
        acc0_refs[j][i] = _dot(mvt[streams[j][0]], p)

    for j in range(len(streams)):
        for par in range(min(2, n_q)):
            acc_refs[j][par] = acc0_refs[j][par]

    def stage_b(j, qi, t, slot, m_prev):
        par = qi & 1
        m_old = jnp.where(t == 0, m0_refs[j][qi], m_prev)
        m_new = jnp.maximum(m_old, tmax_refs[j][slot][...])
        p = jnp.exp2(s_refs[j][slot][...] - m_new).astype(BF16)
        acc_refs[j][par] = (jnp.exp2(m_old - m_new) * acc_refs[j][par]
                            + _dot(vt_ref[streams[j][0], t], p))
        return m_new

    def stages(qa, ta, slot_a, masked, qb, tb, slot_b, ms):
        out = []
        for j in range(len(streams)):
            stage_a(j, qa, ta, slot_a, masked)
            out.append(stage_b(j, qb, tb, slot_b, ms[j]))
        return tuple(out)

    def finish_query_tile(qi):
        par = qi & 1
        r0 = pl.multiple_of(qi * tq, tq)
        nxt = jnp.minimum(qi + 2, n_q - 1)
        for hd in range(n_heads):
            a1, a2 = acc_refs[2 * hd], acc_refs[2 * hd + 1]
            w1 = 1.0 / a1[par, V_DIM:V_DIM + 1, :]
            w2 = _lam(lam_ref, lam_init) / a2[par, V_DIM:V_DIM + 1, :]
            o_t = a1[par, 0:V_DIM, :] * w1 - a2[par, 0:V_DIM, :] * w2
            scale = lax.rsqrt(jnp.mean(o_t * o_t, axis=0, keepdims=True) + RMS_EPS) * (1.0 - lam_init)
            o_ref[pl.ds(r0, tq), head_lanes(hd)] = ((o_t * scale).T * g_ref[...]).astype(BF16)
            for j in (2 * hd, 2 * hd + 1):
                acc_refs[j][par] = acc0_refs[j][nxt]

    def advance(qi, t):
        end = t + 1 == (qi + 1) * ratio
        more = qi + 1 < n_q
        return (jnp.where(end & more, qi + 1, qi),
                jnp.where(end, jnp.where(more, 0, t), t + 1))

    def iteration(slot_b, carry):
        qa, ta, qb, tb, ms = carry

        def run(masked, finish_previous):
            def f():
                if finish_previous:
                    finish_query_tile(qb - 1)
                return stages(qa, ta, 1 - slot_b, masked, qb, tb, slot_b, ms)
            return f

        masked = ta >= qa * ratio
        ms = lax.cond((tb == 0) & (qb > 0),
                      lambda: lax.cond(masked, run(True, True), run(False, True)),
                      lambda: lax.cond(masked, run(True, False), run(False, False)))
        qn, tn = advance(qa, ta)
        return qn, tn, qa, ta, ms

    zero = jnp.int32(0)
    for j in range(len(streams)):
        stage_a(j, zero, zero, 0, True)
    q1_, t1_ = advance(zero, zero)
    m_init = tuple(jnp.full((1, tq), NEG, F32) for _ in streams)
    carry = lax.fori_loop(0, n_items // 2, lambda i, c: iteration(1, iteration(0, c)),
                          (q1_, t1_, zero, zero, m_init))
    if n_items % 2:
        iteration(0, carry)
    finish_query_tile(jnp.int32(n_q - 1))


def _attention(qt, ka, vt, mk, mv, lam_vecs, g, *, batch, seq, lam_init):
    tq, tk = qt.shape[-1], vt.shape[-1]
    n_q, n_k = seq // tq, seq // tk
    hp = HEADS_PER_STEP
    assert seq % tq == 0 and tq % tk == 0 and tk % LANES == 0 and N_HEADS % hp == 0
    assert qt.shape[2] == batch * n_q and vt.shape[1] == batch * n_k
    full = lambda a: pl.BlockSpec(a.shape, lambda b, h: (0,) * a.ndim)
    meta_spec = pl.BlockSpec((N_META, hp * V_DIM), lambda b, h: (0, h))
    vmem = pltpu.VMEM
    per_stream = [vmem((n_q, 1, tq), F32), vmem((n_q, ACC_ROWS, tq), F32),
                  vmem((tk, tq), F32), vmem((tk, tq), F32), vmem((1, tq), F32), vmem((1, tq), F32),
                  vmem((2, ACC_ROWS, tq), F32)]
    return pl.pallas_call(
        functools.partial(_attn_body, lam_init=lam_init),
        grid=(batch, N_HEADS // hp),
        in_specs=[pl.BlockSpec((hp, 2, n_q, LANES, tq), lambda b, h: (h, 0, b, 0, 0)),
                  pl.BlockSpec((seq, hp * 2 * LANES), lambda b, h: (b, h)),
                  pl.BlockSpec((hp, n_k, ACC_ROWS, tk), lambda b, h: (h, b, 0, 0)),
                  meta_spec, meta_spec, full(lam_vecs), full(g)],
        out_specs=pl.BlockSpec((seq, hp * V_DIM), lambda b, h: (b, h)),
        out_shape=jax.ShapeDtypeStruct((batch * seq, N_HEADS * V_DIM), BF16),
        scratch_shapes=[vmem((tq // tk, tk, tq), F32)] + per_stream * (2 * hp),
        compiler_params=pltpu.CompilerParams(dimension_semantics=("arbitrary", "arbitrary"),
                                             vmem_limit_bytes=VMEM_LIMIT),
        name="diff_attention",
    )(qt, ka, vt, mk, mv, lam_vecs, g)


def _meta_attn_body(q_ref, k_ref, v_ref, slope_ref, lam_ref, g_ref, o_ref, *, lam_init):
    q = q_ref[...]
    k = k_ref[...]
    n = q.shape[0]
    lane = lax.broadcasted_iota(jnp.int32, q.shape, 1)
    zero = jnp.zeros_like(q)
    contract_last = (((1,), (1,)), ((), ()))
    s1 = lax.dot_general(jnp.where(lane < HEAD_DIM, q, zero), k, contract_last,
                         preferred_element_type=F32)
    s2 = lax.dot_general(jnp.where(lane >= HEAD_DIM, q, zero), k, contract_last,
                         preferred_element_type=F32)
    qpos = lax.broadcasted_iota(jnp.int32, (n, n), 0)
    kpos = lax.broadcasted_iota(jnp.int32, (n, n), 1)
    bias = -slope_ref[...][:, 0:1] * (qpos - kpos).astype(F32)

    def softmax(s):
        s = jnp.where(kpos <= qpos, s + bias, NEG)
        e = jnp.exp(s - jnp.max(s, axis=-1, keepdims=True))
        return e / jnp.sum(e, axis=-1, keepdims=True)

    w = softmax(s1) - _lam(lam_ref, lam_init) * softmax(s2)
    o_ref[...] = _subln(_dot(w.astype(BF16), v_ref[...]), g_ref[...], lam_init).astype(BF16)


def _meta_attention(q, k, v, slopes, lam_vecs, g, *, lam_init):
    full = lambda a: pl.BlockSpec(a.shape, lambda h: (0,) * a.ndim)
    head = pl.BlockSpec((N_META, V_DIM), lambda h: (0, h))
    return pl.pallas_call(
        functools.partial(_meta_attn_body, lam_init=lam_init),
        grid=(N_HEADS,),
        in_specs=[head, head, head, pl.BlockSpec((None, 1, LANES), lambda h: (h, 0, 0)),
                  full(lam_vecs), full(g)],
        out_specs=head,
        out_shape=jax.ShapeDtypeStruct(q.shape, BF16),
        name="meta_attention",
    )(q, k, v, slopes, lam_vecs, g)


def _merge_body(ypre_ref, o_ref, sga_ref, sgb_ref, h_ref, wc_ref, wa_ref, wm_ref, g_ref, out_ref,
                gated_ref, acc_ref):
    n_chunks = wm_ref.shape[0] // COL_CHUNK
    chunk = lambda j: slice(j * COL_CHUNK, (j + 1) * COL_CHUNK)

    def branches(j):
        ya = _dot(ypre_ref[...], wc_ref[:, chunk(j)])
        yb = _dot(o_ref[...], wa_ref[:, chunk(j)])
        gated = sga_ref[:, chunk(j)].astype(F32) * ya + sgb_ref[:, chunk(j)].astype(F32) * yb
        gated_ref[j % 2] = gated.astype(BF16)

    def mix(j):
        y = _dot(gated_ref[j % 2], wm_ref[chunk(j), :])
        if j == 0:
            acc_ref[...] = y
        else:
            acc_ref[...] += y

    branches(0)
    for j in range(n_chunks):
        if j + 1 < n_chunks:
            branches(j + 1)
        mix(j)
    out_ref[...] = h_ref[...] + _rms(acc_ref[...], g_ref[...])


def _merge(ypre, o, sga, sgb, h, wc, wa, wm, g):
    rows, d = h.shape
    tm = min(ROW_TILE, rows)
    assert rows % tm == 0
    row_spec = lambda n: pl.BlockSpec((tm, n), lambda i: (i, 0))
    return pl.pallas_call(
        _merge_body,
        grid=(rows // tm,),
        in_specs=[row_spec(ypre.shape[1]), row_spec(o.shape[1]), row_spec(sga.shape[1]),
                  row_spec(sgb.shape[1]), row_spec(d),
                  _resident(wc.shape), _resident(wa.shape), _resident(wm.shape), _resident(g.shape)],
        out_specs=row_spec(d),
        out_shape=jax.ShapeDtypeStruct((rows, d), F32),
        scratch_shapes=[pltpu.VMEM((2, tm, COL_CHUNK), BF16), pltpu.VMEM((tm, d), F32)],
        compiler_params=pltpu.CompilerParams(dimension_semantics=("arbitrary",),
                                             vmem_limit_bytes=VMEM_LIMIT),
        name="merge",
    )(ypre, o, sga, sgb, h, wc, wa, wm, g)


def _ffn_body(h_ref, gpre_ref, wup_ref, cw_ref, cb_ref, hist_ref, wdn_ref, gpost_ref,
              out_ref, tail_ref, f_ref, buf_ref, carry_ref, acc_ref, a_ref, *, tiles_per_seq, d_ff):
    @pl.when(pl.program_id(0) % tiles_per_seq == 0)
    def _():
        carry_ref[...] = hist_ref[...]

    tm = h_ref.shape[0]
    n_chunks = d_ff // COL_CHUNK
    half_cols = lambda c, half: slice(half * d_ff + c * COL_CHUNK, half * d_ff + (c + 1) * COL_CHUNK)
    half_buf = lambda c, half: buf_ref.at[(2 * c + half) % CONV_BUFFERS]

    def norm_rows(rows):
        f_ref[rows, :] = _rms(h_ref[rows, :], gpre_ref[...]).astype(BF16)

    head_rows = tm // 2 if tm % 32 == 0 else tm
    norm_rows(slice(0, head_rows))

    def up(c):
        for half in range(2):
            cols = half_cols(c, half)
            if c == 0 and half == 0 and head_rows < tm:
                top = _dot(f_ref[0:head_rows, :], wup_ref[:, cols])
                norm_rows(slice(head_rows, tm))
                z = jnp.concatenate([top, _dot(f_ref[head_rows:tm, :], wup_ref[:, cols])], axis=0)
            else:
                z = _dot(f_ref[...], wup_ref[:, cols])
            _conv_stage(z, half_buf(c, half), carry_ref, cols)

    def gate(c):
        g, u = (_conv_apply(cw_ref[:, half_cols(c, half)], half_buf(c, half)) + cb_ref[:, half_cols(c, half)]
                for half in range(2))
        a_ref[c % (2 * DOWN_GROUP)] = (jax.nn.silu(g) * u).astype(BF16)

    def down(c0, n, rows=slice(0, tm)):
        a = jnp.concatenate([a_ref[(c0 + i) % (2 * DOWN_GROUP), rows, :] for i in range(n)], axis=1)
        y = _dot(a, wdn_ref[c0 * COL_CHUNK:(c0 + n) * COL_CHUNK, :])
        if c0 == 0:
            acc_ref[rows, :] = y
        else:
            acc_ref[rows, :] += y

    def last_down(c0, n):
        parts = [slice(0, head_rows), slice(head_rows, tm)] if head_rows < tm else [slice(0, tm)]
        for rows in parts:
            down(c0, n, rows)
        for rows in parts:
            out_ref[rows, :] = h_ref[rows, :] + _rms(acc_ref[rows, :], gpost_ref[...])

    assert CONV_BUFFERS >= 6
    up(0)
    up(1)
    for c in range(n_chunks + 1):
        if c + 2 < n_chunks:
            up(c + 2)
        if c >= DOWN_GROUP and c % DOWN_GROUP == 0:
            (last_down if c == n_chunks else down)(c - DOWN_GROUP, DOWN_GROUP)
        if c < n_chunks:
            gate(c)
    if n_chunks % DOWN_GROUP:
        last_down(n_chunks - n_chunks % DOWN_GROUP, n_chunks % DOWN_GROUP)

    tail_ref[...] = carry_ref[...]


def _ffn(h, gpre, wup, conv_w, conv_b, hist, wdn, gpost, *, seq_rows):
    rows, d = h.shape
    d_ff = wdn.shape[0]
    tm = min(ROW_TILE, rows)
    assert rows % tm == 0 and seq_rows % tm == 0 and d_ff % COL_CHUNK == 0
    row_spec = pl.BlockSpec((tm, d), lambda i: (i, 0))
    return pl.pallas_call(
        functools.partial(_ffn_body, tiles_per_seq=seq_rows // tm, d_ff=d_ff),
        grid=(rows // tm,),
        in_specs=[row_spec, _resident(gpre.shape), _resident(wup.shape), _resident(conv_w.shape),
                  _resident(conv_b.shape), _resident(hist.shape), _resident(wdn.shape),
                  _resident(gpost.shape)],
        out_specs=[row_spec, pl.BlockSpec((SUBLANES, 2 * d_ff), lambda i: (0, 0))],
        out_shape=[jax.ShapeDtypeStruct((rows, d), F32),
                   jax.ShapeDtypeStruct((SUBLANES, 2 * d_ff), F32)],
        scratch_shapes=[pltpu.VMEM((tm, d), BF16),
                        pltpu.VMEM((CONV_BUFFERS, tm + SUBLANES, COL_CHUNK), F32),
                        pltpu.VMEM((SUBLANES, 2 * d_ff), F32),
                        pltpu.VMEM((tm, d), F32),
                        pltpu.VMEM((2 * DOWN_GROUP, tm, COL_CHUNK), BF16)],
        compiler_params=pltpu.CompilerParams(dimension_semantics=("arbitrary",),
                                             vmem_limit_bytes=VMEM_LIMIT),
        name="ffn",
    )(h, gpre, wup, conv_w, conv_b, hist, wdn, gpost)


def kernel(x, meta_tokens, w_in, conv_w, w_conv_out, lambda_q1, lambda_k1, lambda_q2, lambda_k2,
           subln_g, w_attn_out, w_mix_out, norm_mix_pre, norm_mix_post, w_ffn_up, ffn_conv_w,
           ffn_conv_b, w_ffn_down, norm_ffn_pre, norm_ffn_post):
    batch, seq, d = x.shape
    depth = w_in.shape[0]
    assert meta_tokens.shape[0] == N_META and seq + N_META <= POS_SPLIT * 256 and seq % ROW_TILE == 0
    slopes = jnp.asarray(np.broadcast_to(_alibi_slopes()[:, None, None], (N_HEADS, 1, LANES)), F32)
    row = lambda a: a.reshape(1, -1)

    hx = x.reshape(batch * seq, d)
    hm = meta_tokens.astype(x.dtype)
    for l in range(depth):
        lam_init = 0.8 - 0.6 * math.exp(-0.3 * l)
        w_in_l = w_in[l].astype(BF16)
        lam_vecs = jnp.stack([lambda_q1[l], lambda_k1[l], lambda_q2[l], lambda_k2[l]]).astype(F32)
        g_sub = row(subln_g[l])
        zero_hist = jnp.zeros((SUBLANES, conv_w.shape[2]), F32)

        ypre_m, q_m, k_m, v_m, sga_m, sgb_m, u_tail = _in_proj(
            hm, row(norm_mix_pre[l]), w_in_l, conv_w[l], zero_hist, seq_rows=N_META, attn_layout=False)
        ypre, qt, ka, vt, sga, sgb, _ = _in_proj(
            hx, row(norm_mix_pre[l]), w_in_l, conv_w[l], u_tail, seq_rows=seq, attn_layout=True)

        o_m = _meta_attention(q_m, k_m, v_m, slopes, lam_vecs, g_sub, lam_init=lam_init)
        o = _attention(qt, ka, vt, k_m, v_m, lam_vecs, g_sub, batch=batch, seq=seq, lam_init=lam_init)

        wc, wa, wm = (w.astype(BF16) for w in (w_conv_out[l], w_attn_out[l], w_mix_out[l]))
        hm = _merge(ypre_m, o_m, sga_m, sgb_m, hm, wc, wa, wm, row(norm_mix_post[l]))
        hx = _merge(ypre, o, sga, sgb, hx, wc, wa, wm, row(norm_mix_post[l]))

        wup, wdn = w_ffn_up[l].astype(BF16), w_ffn_down[l].astype(BF16)
        zero_hist = jnp.zeros((SUBLANES, wup.shape[1]), F32)
        ffn = functools.partial(_ffn, gpre=row(norm_ffn_pre[l]), wup=wup, conv_w=ffn_conv_w[l],
                                conv_b=row(ffn_conv_b[l]), wdn=wdn, gpost=row(norm_ffn_post[l]))
        hm, z_tail = ffn(hm, hist=zero_hist, seq_rows=N_META)
        hx, _ = ffn(hx, hist=z_tail, seq_rows=seq)
    return hx.reshape(batch, seq, d)
```

```python
import functools
import math

import numpy as np
import jax
import jax.numpy as jnp
from jax import lax
from jax.experimental import pallas as pl
from jax.experimental.pallas import tpu as pltpu

F32 = jnp.float32
BF16 = jnp.bfloat16

N_META = 16
N_HEADS = 8
HEAD_DIM = 64
V_DIM = 2 * HEAD_DIM
N_SEG = 8
RMS_EPS = 1e-6
NEG = -1e30

LANES = 128
SUBLANES = 8
ROW_TILE = 512
COL_CHUNK = 256
CONV_BUFFERS = 6
DOWN_GROUP = 2
Q_TILE = ROW_TILE
K_TILE = ROW_TILE
ACC_ROWS = V_DIM + 16
HEADS_PER_STEP = 2
VMEM_LIMIT = 56 * 1024 * 1024

POS_SPLIT = 64
COEF_PARTS = 3
LANE_POS_HI = HEAD_DIM
LANE_POS_LO = HEAD_DIM + COEF_PARTS
LOG2E = math.log2(math.e)


def _rms(x, g):
    ms = jnp.mean(x * x, axis=-1, keepdims=True)
    return x * lax.rsqrt(ms + RMS_EPS) * g


def _dot(a, b):
    return jnp.dot(a, b, preferred_element_type=F32)


def _conv_stage(z, buf_ref, carry_ref, cols):
    tm = z.shape[0]
    buf_ref[0:SUBLANES, :] = carry_ref[:, cols]
    buf_ref[SUBLANES:SUBLANES + tm, :] = z
    carry_ref[:, cols] = z[tm - SUBLANES:tm]


def _conv_apply(w, buf_ref):
    tm = buf_ref.shape[0] - SUBLANES
    return (w[2:3] * buf_ref[SUBLANES:SUBLANES + tm, :]
            + w[1:2] * buf_ref[SUBLANES - 1:SUBLANES - 1 + tm, :]
            + w[0:1] * buf_ref[SUBLANES - 2:SUBLANES - 2 + tm, :])


def _causal_conv3(z, w, buf_ref, carry_ref, cols):
    _conv_stage(z, buf_ref, carry_ref, cols)
    return _conv_apply(w, buf_ref)


def _alibi_slopes():
    return 2.0 ** (-8.0 * np.arange(1, N_HEADS + 1, dtype=np.float64) / N_HEADS)


def _bf16_parts(value):
    parts, rest = [], np.float64(value)
    for _ in range(COEF_PARTS):
        part = np.float64(rest.astype(BF16))
        parts.append(float(part))
        rest = rest - part
    return parts


def _lane_select(lane, first_lane, values, otherwise):
    out = otherwise
    for j, value in enumerate(values):
        out = jnp.where(lane == first_lane + j, value, out)
    return out


def _key_features(pos0, n):
    pos = lax.broadcasted_iota(jnp.int32, (n, LANES), 0) + pos0
    lane = lax.broadcasted_iota(jnp.int32, (n, LANES), 1)
    hi = (pos >> 6).astype(F32)
    lo = (pos & (POS_SPLIT - 1)).astype(F32)
    in_hi = (lane >= LANE_POS_HI) & (lane < LANE_POS_HI + COEF_PARTS)
    in_lo = (lane >= LANE_POS_LO) & (lane < LANE_POS_LO + COEF_PARTS)
    return jnp.where(in_hi, hi, jnp.where(in_lo, lo, 0.0)), lane


def _query_features(slope, lane):
    parts = _bf16_parts(slope * LOG2E)
    feat = _lane_select(lane, LANE_POS_HI, [p * POS_SPLIT for p in parts], 0.0)
    return _lane_select(lane, LANE_POS_LO, parts, feat)


def _split_maps(x, feat, lane):
    first = jnp.where(lane < HEAD_DIM, x, feat)
    second = jnp.where(lane < HEAD_DIM, pltpu.roll(x, HEAD_DIM, axis=1), feat)
    return first, second


def _ones_rows(n):
    return jnp.where(lax.broadcasted_iota(jnp.int32, (ACC_ROWS - V_DIM, n), 0) == 0, 1.0, 0.0)


def _in_proj_body(x_ref, g_ref, w_ref, cw_ref, hist_ref,
                  ypre_ref, q_ref, k_ref, v_ref, sga_ref, sgb_ref, tail_ref,
                  xn_ref, buf_ref, carry_ref, *, tiles_per_seq, width, q_scale, attn_layout):
    tile_in_seq = pl.program_id(0) % tiles_per_seq

    @pl.when(tile_in_seq == 0)
    def _():
        carry_ref[...] = hist_ref[...]

    xn_ref[...] = _rms(x_ref[...], g_ref[...]).astype(BF16)
    tm = x_ref.shape[0]
    heads_per_chunk = COL_CHUNK // V_DIM
    if attn_layout:
        slopes = _alibi_slopes()
        kfeat, lane = _key_features(N_META + tile_in_seq * tm, tm)

    for c in range(width // COL_CHUNK):
        cols = slice(c * COL_CHUNK, (c + 1) * COL_CHUNK)

        def proj(seg):
            lo = seg * width + c * COL_CHUNK
            return _dot(xn_ref[...], w_ref[:, lo:lo + COL_CHUNK])

        u = proj(1) * proj(2)
        y = _causal_conv3(u, cw_ref[:, cols], buf_ref.at[c % CONV_BUFFERS], carry_ref, cols)
        ypre_ref[:, cols] = (proj(0) * y).astype(BF16)
        sga_ref[:, cols] = jax.nn.sigmoid(proj(6)).astype(BF16)
        sgb_ref[:, cols] = jax.nn.sigmoid(proj(7)).astype(BF16)
        q, k, v = proj(3) * q_scale, proj(4), proj(5)
        if not attn_layout:
            q_ref[:, cols] = q.astype(BF16)
            k_ref[:, cols] = k.astype(BF16)
            v_ref[:, cols] = v.astype(BF16)
            continue
        for hh in range(heads_per_chunk):
            head = c * heads_per_chunk + hh
            lanes = slice(hh * V_DIM, (hh + 1) * V_DIM)
            qfeat = _query_features(float(slopes[head]), lane)
            for m, qm in enumerate(_split_maps(q[:, lanes], qfeat, lane)):
                q_ref[head, m, 0] = qm.T.astype(BF16)
            for m, km in enumerate(_split_maps(k[:, lanes], kfeat, lane)):
                k_ref[:, (2 * head + m) * LANES:(2 * head + m + 1) * LANES] = km.astype(BF16)
            v_ref[head, 0, 0:V_DIM, :] = v[:, lanes].T.astype(BF16)
            v_ref[head, 0, V_DIM:ACC_ROWS, :] = _ones_rows(tm).astype(BF16)

    tail_ref[...] = carry_ref[...]


def _resident(shape):
    return pl.BlockSpec(shape, lambda i: (0,) * len(shape), pipeline_mode=pl.Buffered(1))


def _in_proj(h, gain, w_in, conv_w, hist, *, seq_rows, attn_layout):
    rows, d = h.shape
    width = w_in.shape[1] // N_SEG
    tm = min(ROW_TILE, rows)
    n_tiles = rows // tm
    assert rows % tm == 0 and seq_rows % tm == 0 and width % COL_CHUNK == 0
    assert width == N_HEADS * V_DIM and COL_CHUNK % V_DIM == 0
    row_spec = lambda n: pl.BlockSpec((tm, n), lambda i: (i, 0))
    act = jax.ShapeDtypeStruct((rows, width), BF16)
    if attn_layout:
        qkv_specs = [pl.BlockSpec((N_HEADS, 2, 1, LANES, tm), lambda i: (0, 0, i, 0, 0)),
                     row_spec(2 * width),
                     pl.BlockSpec((N_HEADS, 1, ACC_ROWS, tm), lambda i: (0, i, 0, 0))]
        qkv_shapes = [jax.ShapeDtypeStruct((N_HEADS, 2, n_tiles, LANES, tm), BF16),
                      jax.ShapeDtypeStruct((rows, 2 * width), BF16),
                      jax.ShapeDtypeStruct((N_HEADS, n_tiles, ACC_ROWS, tm), BF16)]
    else:
        qkv_specs, qkv_shapes = [row_spec(width)] * 3, [act] * 3
    return pl.pallas_call(
        functools.partial(_in_proj_body, tiles_per_seq=seq_rows // tm, width=width,
                          q_scale=HEAD_DIM ** -0.5 * (LOG2E if attn_layout else 1.0),
                          attn_layout=attn_layout),
        grid=(n_tiles,),
        in_specs=[row_spec(d), _resident((1, d)), _resident(w_in.shape), _resident(conv_w.shape),
                  _resident(hist.shape)],
        out_specs=[row_spec(width)] + qkv_specs + [row_spec(width)] * 2
                  + [pl.BlockSpec((SUBLANES, width), lambda i: (0, 0))],
        out_shape=[act] + qkv_shapes + [act] * 2 + [jax.ShapeDtypeStruct((SUBLANES, width), F32)],
        scratch_shapes=[pltpu.VMEM((tm, d), BF16),
                        pltpu.VMEM((CONV_BUFFERS, tm + SUBLANES, COL_CHUNK), F32),
                        pltpu.VMEM((SUBLANES, width), F32)],
        compiler_params=pltpu.CompilerParams(dimension_semantics=("arbitrary",),
                                             vmem_limit_bytes=VMEM_LIMIT),
        name="in_proj",
    )(h, gain, w_in, conv_w, hist)


def _lam(lam_ref, lam_init):
    l = lam_ref[...]
    return (jnp.exp(jnp.sum(l[0:1] * l[1:2], axis=-1, keepdims=True))
            - jnp.exp(jnp.sum(l[2:3] * l[3:4], axis=-1, keepdims=True)) + lam_init)


def _subln(o, g, lam_init):
    return _rms(o, g) * (1.0 - lam_init)


def _attn_body(qt_ref, ka_ref, vt_ref, mk_ref, mv_ref, lam_ref, g_ref, o_ref, mask_ref, *scratch,
               lam_init):
    n_heads, _, n_q, _, tq = qt_ref.shape
    n_k, _, tk = vt_ref.shape[1:]
    ratio = tq // tk
    n_items = ratio * n_q * (n_q + 1) // 2
    streams = [(hd, m) for hd in range(n_heads) for m in range(2)]
    m0_refs, acc0_refs, s_refs, tmax_refs, acc_refs = [], [], [], [], []
    for j in range(len(streams)):
        m0, acc0, s_a, s_b, tm_a, tm_b, acc = scratch[7 * j:7 * j + 7]
        m0_refs.append(m0)
        acc0_refs.append(acc0)
        s_refs.append((s_a, s_b))
        tmax_refs.append((tm_a, tm_b))
        acc_refs.append(acc)
    head_lanes = lambda hd: slice(hd * V_DIM, (hd + 1) * V_DIM)

    @pl.when((pl.program_id(0) == 0) & (pl.program_id(1) == 0))
    def _build_masks():
        kk = lax.broadcasted_iota(jnp.int32, (tk, tq), 0)
        qq = lax.broadcasted_iota(jnp.int32, (tk, tq), 1)
        for d in range(ratio):
            mask_ref[d] = jnp.where(kk + d * tk <= qq, 0.0, NEG)

    def stage_a(j, qi, t, slot, masked):
        hd, m = streams[j]
        r0 = pl.multiple_of(t * tk, tk)
        keys = ka_ref[pl.ds(r0, tk), (2 * hd + m) * LANES:(2 * hd + m + 1) * LANES]
        s = _dot(keys, qt_ref[hd, m, qi])
        if masked:
            s = s + mask_ref[jnp.maximum(t - qi * ratio, 0)]
        s_refs[j][slot][...] = s
        tmax_refs[j][slot][...] = jnp.max(s, axis=0, keepdims=True)

    feat, lane = _key_features(0, N_META)
    mk_aug, mvt = [], []
    for hd in range(n_heads):
        mk_aug.extend(a.astype(BF16) for a in _split_maps(mk_ref[:, head_lanes(hd)].astype(F32), feat, lane))
        mv_pad = jnp.concatenate([mv_ref[:, head_lanes(hd)].astype(F32),
                                  jnp.zeros((LANES - N_META, LANES), F32)], axis=0)
        mvt.append(jnp.concatenate([mv_pad.T[:, 0:N_META], _ones_rows(N_META)], axis=0).astype(BF16))

    pairs = [(j, i) for j in range(len(streams)) for i in range(n_q)]
    s_meta = [_dot(mk_aug[j], qt_ref[streams[j][0], streams[j][1], i]) for j, i in pairs]
    m0 = [jnp.max(s, axis=0, keepdims=True) for s in s_meta]
    p_meta = [jnp.exp2(s - mx).astype(BF16) for s, mx in zip(s_meta, m0)]
    for (j, i), mx, p in zip(pairs, m0, p_meta):
        m0_refs[j][i] = mx
        acc0_refs[j][i] = _dot(mvt[streams[j][0]], p)

    for j in range(len(streams)):
        for par in range(min(2, n_q)):
            acc_refs[j][par] = acc0_refs[j][par]

    def stage_b(j, qi, t, slot, m_prev):
        par = qi & 1
        m_old = jnp.where(t == 0, m0_refs[j][qi], m_prev)
        m_new = jnp.maximum(m_old, tmax_refs[j][slot][...])
        p = jnp.exp2(s_refs[j][slot][...] - m_new).astype(BF16)
        acc_refs[j][par] = (jnp.exp2(m_old - m_new) * acc_refs[j][par]
                            + _dot(vt_ref[streams[j][0], t], p))
        return m_new

    def stages(qa, ta, slot_a, masked, qb, tb, slot_b, ms):
        out = []
        for j in range(len(streams)):
            stage_a(j, qa, ta, slot_a, masked)
            out.append(stage_b(j, qb, tb, slot_b, ms[j]))
        return tuple(out)

    def finish_query_tile(qi):
        par = qi & 1
        r0 = pl.multiple_of(qi * tq, tq)
        nxt = jnp.minimum(qi + 2, n_q - 1)
        for hd in range(n_heads):
            a1, a2 = acc_refs[2 * hd], acc_refs[2 * hd + 1]
            w1 = 1.0 / a1[par, V_DIM:V_DIM + 1, :]
            w2 = _lam(lam_ref, lam_init) / a2[par, V_DIM:V_DIM + 1, :]
            o_t = a1[par, 0:V_DIM, :] * w1 - a2[par, 0:V_DIM, :] * w2
            scale = lax.rsqrt(jnp.mean(o_t * o_t, axis=0, keepdims=True) + RMS_EPS) * (1.0 - lam_init)
            o_ref[pl.ds(r0, tq), head_lanes(hd)] = ((o_t * scale).T * g_ref[...]).astype(BF16)
            for j in (2 * hd, 2 * hd + 1):
                acc_refs[j][par] = acc0_refs[j][nxt]

    def advance(qi, t):
        end = t + 1 == (qi + 1) * ratio
        more = qi + 1 < n_q
        return (jnp.where(end & more, qi + 1, qi),
                jnp.where(end, jnp.where(more, 0, t), t + 1))

    def iteration(slot_b, carry):
        qa, ta, qb, tb, ms = carry

        def run(masked, finish_previous):
            def f():
                if finish_previous:
                    finish_query_tile(qb - 1)
                return stages(qa, ta, 1 - slot_b, masked, qb, tb, slot_b, ms)
            return f

        masked = ta >= qa * ratio
        ms = lax.cond((tb == 0) & (qb > 0),
                      lambda: lax.cond(masked, run(True, True), run(False, True)),
                      lambda: lax.cond(masked, run(True, False), run(False, False)))
        qn, tn = advance(qa, ta)
        return qn, tn, qa, ta, ms

    zero = jnp.int32(0)
    for j in range(len(streams)):
        stage_a(j, zero, zero, 0, True)
    q1_, t1_ = advance(zero, zero)
    m_init = tuple(jnp.full((1, tq), NEG, F32) for _ in streams)
    carry = lax.fori_loop(0, n_items // 2, lambda i, c: iteration(1, iteration(0, c)),
                          (q1_, t1_, zero, zero, m_init))
    if n_items % 2:
        iteration(0, carry)
    finish_query_tile(jnp.int32(n_q - 1))


def _attention(qt, ka, vt, mk, mv, lam_vecs, g, *, batch, seq, lam_init):
    tq, tk = qt.shape[-1], vt.shape[-1]
    n_q, n_k = seq // tq, seq // tk
    hp = HEADS_PER_STEP
    assert seq % tq == 0 and tq % tk == 0 and tk % LANES == 0 and N_HEADS % hp == 0
    assert qt.shape[2] == batch * n_q and vt.shape[1] == batch * n_k
    full = lambda a: pl.BlockSpec(a.shape, lambda b, h: (0,) * a.ndim)
    meta_spec = pl.BlockSpec((N_META, hp * V_DIM), lambda b, h: (0, h))
    vmem = pltpu.VMEM
    per_stream = [vmem((n_q, 1, tq), F32), vmem((n_q, ACC_ROWS, tq), F32),
                  vmem((tk, tq), F32), vmem((tk, tq), F32), vmem((1, tq), F32), vmem((1, tq), F32),
                  vmem((2, ACC_ROWS, tq), F32)]
    return pl.pallas_call(
        functools.partial(_attn_body, lam_init=lam_init),
        grid=(batch, N_HEADS // hp),
        in_specs=[pl.BlockSpec((hp, 2, n_q, LANES, tq), lambda b, h: (h, 0, b, 0, 0)),
                  pl.BlockSpec((seq, hp * 2 * LANES), lambda b, h: (b, h)),
                  pl.BlockSpec((hp, n_k, ACC_ROWS, tk), lambda b, h: (h, b, 0, 0)),
                  meta_spec, meta_spec, full(lam_vecs), full(g)],
        out_specs=pl.BlockSpec((seq, hp * V_DIM), lambda b, h: (b, h)),
        out_shape=jax.ShapeDtypeStruct((batch * seq, N_HEADS * V_DIM), BF16),
        scratch_shapes=[vmem((tq // tk, tk, tq), F32)] + per_stream * (2 * hp),
        compiler_params=pltpu.CompilerParams(dimension_semantics=("arbitrary", "arbitrary"),
                                             vmem_limit_bytes=VMEM_LIMIT),
        name="diff_attention",
    )(qt, ka, vt, mk, mv, lam_vecs, g)


def _meta_attn_body(q_ref, k_ref, v_ref, slope_ref, lam_ref, g_ref, o_ref, *, lam_init):
    q = q_ref[...]
    k = k_ref[...]
    n = q.shape[0]
    lane = lax.broadcasted_iota(jnp.int32, q.shape, 1)
    zero = jnp.zeros_like(q)
    contract_last = (((1,), (1,)), ((), ()))
    s1 = lax.dot_general(jnp.where(lane < HEAD_DIM, q, zero), k, contract_last,
                         preferred_element_type=F32)
    s2 = lax.dot_general(jnp.where(lane >= HEAD_DIM, q, zero), k, contract_last,
                         preferred_element_type=F32)
    qpos = lax.broadcasted_iota(jnp.int32, (n, n), 0)
    kpos = lax.broadcasted_iota(jnp.int32, (n, n), 1)
    bias = -slope_ref[...][:, 0:1] * (qpos - kpos).astype(F32)

    def softmax(s):
        s = jnp.where(kpos <= qpos, s + bias, NEG)
        e = jnp.exp(s - jnp.max(s, axis=-1, keepdims=True))
        return e / jnp.sum(e, axis=-1, keepdims=True)

    w = softmax(s1) - _lam(lam_ref, lam_init) * softmax(s2)
    o_ref[...] = _subln(_dot(w.astype(BF16), v_ref[...]), g_ref[...], lam_init).astype(BF16)


def _meta_attention(q, k, v, slopes, lam_vecs, g, *, lam_init):
    full = lambda a: pl.BlockSpec(a.shape, lambda h: (0,) * a.ndim)
    head = pl.BlockSpec((N_META, V_DIM), lambda h: (0, h))
    return pl.pallas_call(
        functools.partial(_meta_attn_body, lam_init=lam_init),
        grid=(N_HEADS,),
        in_specs=[head, head, head, pl.BlockSpec((None, 1, LANES), lambda h: (h, 0, 0)),
                  full(lam_vecs), full(g)],
        out_specs=head,
        out_shape=jax.ShapeDtypeStruct(q.shape, BF16),
        name="meta_attention",
    )(q, k, v, slopes, lam_vecs, g)


def _merge_body(ypre_ref, o_ref, sga_ref, sgb_ref, h_ref, wc_ref, wa_ref, wm_ref, g_ref, out_ref,
                gated_ref, acc_ref):
    n_chunks = wm_ref.shape[0] // COL_CHUNK
    chunk = lambda j: slice(j * COL_CHUNK, (j + 1) * COL_CHUNK)

    def branches(j):
        ya = _dot(ypre_ref[...], wc_ref[:, chunk(j)])
        yb = _dot(o_ref[...], wa_ref[:, chunk(j)])
        gated = sga_ref[:, chunk(j)].astype(F32) * ya + sgb_ref[:, chunk(j)].astype(F32) * yb
        gated_ref[j % 2] = gated.astype(BF16)

    def mix(j):
        y = _dot(gated_ref[j % 2], wm_ref[chunk(j), :])
        if j == 0:
            acc_ref[...] = y
        else:
            acc_ref[...] += y

    branches(0)
    for j in range(n_chunks):
        if j + 1 < n_chunks:
            branches(j + 1)
        mix(j)
    out_ref[...] = h_ref[...] + _rms(acc_ref[...], g_ref[...])


def _merge(ypre, o, sga, sgb, h, wc, wa, wm, g):
    rows, d = h.shape
    tm = min(ROW_TILE, rows)
    assert rows % tm == 0
    row_spec = lambda n: pl.BlockSpec((tm, n), lambda i: (i, 0))
    return pl.pallas_call(
        _merge_body,
        grid=(rows // tm,),
        in_specs=[row_spec(ypre.shape[1]), row_spec(o.shape[1]), row_spec(sga.shape[1]),
                  row_spec(sgb.shape[1]), row_spec(d),
                  _resident(wc.shape), _resident(wa.shape), _resident(wm.shape), _resident(g.shape)],
        out_specs=row_spec(d),
        out_shape=jax.ShapeDtypeStruct((rows, d), F32),
        scratch_shapes=[pltpu.VMEM((2, tm, COL_CHUNK), BF16), pltpu.VMEM((tm, d), F32)],
        compiler_params=pltpu.CompilerParams(dimension_semantics=("arbitrary",),
                                             vmem_limit_bytes=VMEM_LIMIT),
        name="merge",
    )(ypre, o, sga, sgb, h, wc, wa, wm, g)


def _ffn_body(h_ref, gpre_ref, wup_ref, cw_ref, cb_ref, hist_ref, wdn_ref, gpost_ref,
              out_ref, tail_ref, f_ref, buf_ref, carry_ref, acc_ref, a_ref, *, tiles_per_seq, d_ff):
    @pl.when(pl.program_id(0) % tiles_per_seq == 0)
    def _():
        carry_ref[...] = hist_ref[...]

    tm = h_ref.shape[0]
    n_chunks = d_ff // COL_CHUNK
    half_cols = lambda c, half: slice(half * d_ff + c * COL_CHUNK, half * d_ff + (c + 1) * COL_CHUNK)
    half_buf = lambda c, half: buf_ref.at[(2 * c + half) % CONV_BUFFERS]

    def norm_rows(rows):
        f_ref[rows, :] = _rms(h_ref[rows, :], gpre_ref[...]).astype(BF16)

    head_rows = tm // 2 if tm % 32 == 0 else tm
    norm_rows(slice(0, head_rows))

    def up(c):
        for half in range(2):
            cols = half_cols(c, half)
            if c == 0 and half == 0 and head_rows < tm:
                top = _dot(f_ref[0:head_rows, :], wup_ref[:, cols])
                norm_rows(slice(head_rows, tm))
                z = jnp.concatenate([top, _dot(f_ref[head_rows:tm, :], wup_ref[:, cols])], axis=0)
            else:
                z = _dot(f_ref[...], wup_ref[:, cols])
            _conv_stage(z, half_buf(c, half), carry_ref, cols)

    def gate(c):
        g, u = (_conv_apply(cw_ref[:, half_cols(c, half)], half_buf(c, half)) + cb_ref[:, half_cols(c, half)]
                for half in range(2))
        a_ref[c % (2 * DOWN_GROUP)] = (jax.nn.silu(g) * u).astype(BF16)

    def down(c0, n, rows=slice(0, tm)):
        a = jnp.concatenate([a_ref[(c0 + i) % (2 * DOWN_GROUP), rows, :] for i in range(n)], axis=1)
        y = _dot(a, wdn_ref[c0 * COL_CHUNK:(c0 + n) * COL_CHUNK, :])
        if c0 == 0:
            acc_ref[rows, :] = y
        else:
            acc_ref[rows, :] += y

    def last_down(c0, n):
        parts = [slice(0, head_rows), slice(head_rows, tm)] if head_rows < tm else [slice(0, tm)]
        for rows in parts:
            down(c0, n, rows)
        for rows in parts:
            out_ref[rows, :] = h_ref[rows, :] + _rms(acc_ref[rows, :], gpost_ref[...])

    assert CONV_BUFFERS >= 6
    up(0)
    up(1)
    for c in range(n_chunks + 1):
        if c + 2 < n_chunks:
            up(c + 2)
        if c >= DOWN_GROUP and c % DOWN_GROUP == 0:
            (last_down if c == n_chunks else down)(c - DOWN_GROUP, DOWN_GROUP)
        if c < n_chunks:
            gate(c)
    if n_chunks % DOWN_GROUP:
        last_down(n_chunks - n_chunks % DOWN_GROUP, n_chunks % DOWN_GROUP)

    tail_ref[...] = carry_ref[...]


def _ffn(h, gpre, wup, conv_w, conv_b, hist, wdn, gpost, *, seq_rows):
    rows, d = h.shape
    d_ff = wdn.shape[0]
    tm = min(ROW_TILE, rows)
    assert rows % tm == 0 and seq_rows % tm == 0 and d_ff % COL_CHUNK == 0
    row_spec = pl.BlockSpec((tm, d), lambda i: (i, 0))
    return pl.pallas_call(
        functools.partial(_ffn_body, tiles_per_seq=seq_rows // tm, d_ff=d_ff),
        grid=(rows // tm,),
        in_specs=[row_spec, _resident(gpre.shape), _resident(wup.shape), _resident(conv_w.shape),
                  _resident(conv_b.shape), _resident(hist.shape), _resident(wdn.shape),
                  _resident(gpost.shape)],
        out_specs=[row_spec, pl.BlockSpec((SUBLANES, 2 * d_ff), lambda i: (0, 0))],
        out_shape=[jax.ShapeDtypeStruct((rows, d), F32),
                   jax.ShapeDtypeStruct((SUBLANES, 2 * d_ff), F32)],
        scratch_shapes=[pltpu.VMEM((tm, d), BF16),
                        pltpu.VMEM((CONV_BUFFERS, tm + SUBLANES, COL_CHUNK), F32),
                        pltpu.VMEM((SUBLANES, 2 * d_ff), F32),
                        pltpu.VMEM((tm, d), F32),
                        pltpu.VMEM((2 * DOWN_GROUP, tm, COL_CHUNK), BF16)],
        compiler_params=pltpu.CompilerParams(dimension_semantics=("arbitrary",),
                                             vmem_limit_bytes=VMEM_LIMIT),
        name="ffn",
    )(h, gpre, wup, conv_w, conv_b, hist, wdn, gpost)


N_MERGE_IN, N_FFN_IN = 9, 7


def _merge_ffn_body(*refs, tiles_per_seq, d_ff):
    merge_in = refs[:N_MERGE_IN]
    ffn_in = refs[N_MERGE_IN:N_MERGE_IN + N_FFN_IN]
    out_ref, tail_ref = refs[N_MERGE_IN + N_FFN_IN:N_MERGE_IN + N_FFN_IN + 2]
    gated_ref, macc_ref, h1_ref, f_ref, buf_ref, carry_ref, acc_ref, a_ref = refs[N_MERGE_IN + N_FFN_IN + 2:]
    _merge_body(*merge_in, h1_ref, gated_ref, macc_ref)
    _ffn_body(h1_ref, *ffn_in, out_ref, tail_ref, f_ref, buf_ref, carry_ref, acc_ref, a_ref,
              tiles_per_seq=tiles_per_seq, d_ff=d_ff)


def _merge_ffn(ypre, o, sga, sgb, h, wc, wa, wm, g, gpre, wup, conv_w, conv_b, hist, wdn, gpost, *, seq_rows):
    rows, d = h.shape
    d_ff = wdn.shape[0]
    tm = min(ROW_TILE, rows)
    assert rows % tm == 0 and seq_rows % tm == 0 and d_ff % COL_CHUNK == 0
    row_spec = lambda n: pl.BlockSpec((tm, n), lambda i: (i, 0))
    merge_ops = (ypre, o, sga, sgb, h, wc, wa, wm, g)
    ffn_ops = (gpre, wup, conv_w, conv_b, hist, wdn, gpost)
    assert len(merge_ops) == N_MERGE_IN and len(ffn_ops) == N_FFN_IN
    return pl.pallas_call(
        functools.partial(_merge_ffn_body, tiles_per_seq=seq_rows // tm, d_ff=d_ff),
        grid=(rows // tm,),
        in_specs=[row_spec(a.shape[1]) for a in merge_ops[:5]]
                 + [_resident(a.shape) for a in merge_ops[5:] + ffn_ops],
        out_specs=[row_spec(d), pl.BlockSpec((SUBLANES, 2 * d_ff), lambda i: (0, 0))],
        out_shape=[jax.ShapeDtypeStruct((rows, d), F32),
                   jax.ShapeDtypeStruct((SUBLANES, 2 * d_ff), F32)],
        scratch_shapes=[pltpu.VMEM((2, tm, COL_CHUNK), BF16), pltpu.VMEM((tm, d), F32),
                        pltpu.VMEM((tm, d), F32),
                        pltpu.VMEM((tm, d), BF16),
                        pltpu.VMEM((CONV_BUFFERS, tm + SUBLANES, COL_CHUNK), F32),
                        pltpu.VMEM((SUBLANES, 2 * d_ff), F32),
                        pltpu.VMEM((tm, d), F32),
                        pltpu.VMEM((2 * DOWN_GROUP, tm, COL_CHUNK), BF16)],
        compiler_params=pltpu.CompilerParams(dimension_semantics=("arbitrary",),
                                             vmem_limit_bytes=VMEM_LIMIT),
        name="merge_ffn",
    )(*merge_ops, *ffn_ops)


def kernel(x, meta_tokens, w_in, conv_w, w_conv_out, lambda_q1, lambda_k1, lambda_q2, lambda_k2,
           subln_g, w_attn_out, w_mix_out, norm_mix_pre, norm_mix_post, w_ffn_up, ffn_conv_w,
           ffn_conv_b, w_ffn_down, norm_ffn_pre, norm_ffn_post):
    batch, seq, d = x.shape
    depth = w_in.shape[0]
    assert meta_tokens.shape[0] == N_META and seq + N_META <= POS_SPLIT * 256 and seq % ROW_TILE == 0
    slopes = jnp.asarray(np.broadcast_to(_alibi_slopes()[:, None, None], (N_HEADS, 1, LANES)), F32)
    row = lambda a: a.reshape(1, -1)

    hx = x.reshape(batch * seq, d)
    hm = meta_tokens.astype(x.dtype)
    for l in range(depth):
        lam_init = 0.8 - 0.6 * math.exp(-0.3 * l)
        w_in_l = w_in[l].astype(BF16)
        lam_vecs = jnp.stack([lambda_q1[l], lambda_k1[l], lambda_q2[l], lambda_k2[l]]).astype(F32)
        g_sub = row(subln_g[l])
        zero_hist = jnp.zeros((SUBLANES, conv_w.shape[2]), F32)

        ypre_m, q_m, k_m, v_m, sga_m, sgb_m, u_tail = _in_proj(
            hm, row(norm_mix_pre[l]), w_in_l, conv_w[l], zero_hist, seq_rows=N_META, attn_layout=False)
        ypre, qt, ka, vt, sga, sgb, _ = _in_proj(
            hx, row(norm_mix_pre[l]), w_in_l, conv_w[l], u_tail, seq_rows=seq, attn_layout=True)

        o_m = _meta_attention(q_m, k_m, v_m, slopes, lam_vecs, g_sub, lam_init=lam_init)
        o = _attention(qt, ka, vt, k_m, v_m, lam_vecs, g_sub, batch=batch, seq=seq, lam_init=lam_init)

        wc, wa, wm = (w.astype(BF16) for w in (w_conv_out[l], w_attn_out[l], w_mix_out[l]))
        wup, wdn = w_ffn_up[l].astype(BF16), w_ffn_down[l].astype(BF16)
        zero_hist = jnp.zeros((SUBLANES, wup.shape[1]), F32)

        def merge_ffn(ypre, o, sga, sgb, h, hist, seq_rows):
            return _merge_ffn(ypre, o, sga, sgb, h, wc, wa, wm, row(norm_mix_post[l]),
                              row(norm_ffn_pre[l]), wup, ffn_conv_w[l], row(ffn_conv_b[l]), hist, wdn,
                              row(norm_ffn_post[l]), seq_rows=seq_rows)

        hm, z_tail = merge_ffn(ypre_m, o_m, sga_m, sgb_m, hm, zero_hist, N_META)
        hx, _ = merge_ffn(ypre, o, sga, sgb, hx, z_tail, seq)
    return hx.reshape(batch, seq, d)
```

```python
import functools
import math

import numpy as np
import jax
import jax.numpy as jnp
from jax import lax
from jax.experimental import pallas as pl
from jax.experimental.pallas import tpu as pltpu

F32 = jnp.float32
BF16 = jnp.bfloat16

N_META = 16
N_HEADS = 8
HEAD_DIM = 64
V_DIM = 2 * HEAD_DIM
N_SEG = 8
RMS_EPS = 1e-6
NEG = -1e30

LANES = 128
SUBLANES = 8
ROW_TILE = 512
COL_CHUNK = 256
CONV_BUFFERS = 6
DOWN_GROUP = 2
Q_TILE = ROW_TILE
K_TILE = ROW_TILE
ACC_ROWS = V_DIM + 16
HEADS_PER_STEP = 2
VMEM_LIMIT = 56 * 1024 * 1024

POS_SPLIT = 64
COEF_PARTS = 3
LANE_POS_HI = HEAD_DIM
LANE_POS_LO = HEAD_DIM + COEF_PARTS
LOG2E = math.log2(math.e)


def _rms(x, g):
    ms = jnp.mean(x * x, axis=-1, keepdims=True)
    return x * lax.rsqrt(ms + RMS_EPS) * g


def _dot(a, b):
    return jnp.dot(a, b, preferred_element_type=F32)


def _conv_stage(z, buf_ref, carry_ref, cols):
    tm = z.shape[0]
    buf_ref[0:SUBLANES, :] = carry_ref[:, cols]
    buf_ref[SUBLANES:SUBLANES + tm, :] = z
    carry_ref[:, cols] = z[tm - SUBLANES:tm]


def _conv_apply(w, buf_ref):
    tm = buf_ref.shape[0] - SUBLANES
    return (w[2:3] * buf_ref[SUBLANES:SUBLANES + tm, :]
            + w[1:2] * buf_ref[SUBLANES - 1:SUBLANES - 1 + tm, :]
            + w[0:1] * buf_ref[SUBLANES - 2:SUBLANES - 2 + tm, :])


def _causal_conv3(z, w, buf_ref, carry_ref, cols):
    _conv_stage(z, buf_ref, carry_ref, cols)
    return _conv_apply(w, buf_ref)


def _alibi_slopes():
    return 2.0 ** (-8.0 * np.arange(1, N_HEADS + 1, dtype=np.float64) / N_HEADS)


def _bf16_parts(value):
    parts, rest = [], np.float64(value)
    for _ in range(COEF_PARTS):
        part = np.float64(rest.astype(BF16))
        parts.append(float(part))
        rest = rest - part
    return parts


def _lane_select(lane, first_lane, values, otherwise):
    out = otherwise
    for j, value in enumerate(values):
        out = jnp.where(lane == first_lane + j, value, out)
    return out


def _key_features(pos0, n):
    pos = lax.broadcasted_iota(jnp.int32, (n, LANES), 0) + pos0
    lane = lax.broadcasted_iota(jnp.int32, (n, LANES), 1)
    hi = (pos >> 6).astype(F32)
    lo = (pos & (POS_SPLIT - 1)).astype(F32)
    in_hi = (lane >= LANE_POS_HI) & (lane < LANE_POS_HI + COEF_PARTS)
    in_lo = (lane >= LANE_POS_LO) & (lane < LANE_POS_LO + COEF_PARTS)
    return jnp.where(in_hi, hi, jnp.where(in_lo, lo, 0.0)), lane


def _query_features(slope, lane):
    parts = _bf16_parts(slope * LOG2E)
    feat = _lane_select(lane, LANE_POS_HI, [p * POS_SPLIT for p in parts], 0.0)
    return _lane_select(lane, LANE_POS_LO, parts, feat)


def _split_maps(x, feat, lane):
    first = jnp.where(lane < HEAD_DIM, x, feat)
    second = jnp.where(lane < HEAD_DIM, pltpu.roll(x, HEAD_DIM, axis=1), feat)
    return first, second


def _ones_rows(n):
    return jnp.where(lax.broadcasted_iota(jnp.int32, (ACC_ROWS - V_DIM, n), 0) == 0, 1.0, 0.0)


def _in_proj_body(x_ref, g_ref, w_ref, cw_ref, hist_ref,
                  ypre_ref, q_ref, k_ref, v_ref, sga_ref, sgb_ref, tail_ref,
                  xn_ref, buf_ref, carry_ref, *, tiles_per_seq, width, q_scale, attn_layout):
    tile_in_seq = pl.program_id(0) % tiles_per_seq

    @pl.when(tile_in_seq == 0)
    def _():
        carry_ref[...] = hist_ref[...]

    xn_ref[...] = _rms(x_ref[...], g_ref[...]).astype(BF16)
    tm = x_ref.shape[0]
    heads_per_chunk = COL_CHUNK // V_DIM
    if attn_layout:
        slopes = _alibi_slopes()
        kfeat, lane = _key_features(N_META + tile_in_seq * tm, tm)

    for c in range(width // COL_CHUNK):
        cols = slice(c * COL_CHUNK, (c + 1) * COL_CHUNK)

        def proj(seg):
            lo = seg * width + c * COL_CHUNK
            return _dot(xn_ref[...], w_ref[:, lo:lo + COL_CHUNK])

        u = proj(1) * proj(2)
        y = _causal_conv3(u, cw_ref[:, cols], buf_ref.at[c % CONV_BUFFERS], carry_ref, cols)
        ypre_ref[:, cols] = (proj(0) * y).astype(BF16)
        sga_ref[:, cols] = jax.nn.sigmoid(proj(6)).astype(BF16)
        sgb_ref[:, cols] = jax.nn.sigmoid(proj(7)).astype(BF16)
        q, k, v = proj(3) * q_scale, proj(4), proj(5)
        if not attn_layout:
            q_ref[:, cols] = q.astype(BF16)
            k_ref[:, cols] = k.astype(BF16)
            v_ref[:, cols] = v.astype(BF16)
            continue
        for hh in range(heads_per_chunk):
            head = c * heads_per_chunk + hh
            lanes = slice(hh * V_DIM, (hh + 1) * V_DIM)
            qfeat = _query_features(float(slopes[head]), lane)
            for m, qm in enumerate(_split_maps(q[:, lanes], qfeat, lane)):
                q_ref[head, m, 0] = qm.T.astype(BF16)
            for m, km in enumerate(_split_maps(k[:, lanes], kfeat, lane)):
                k_ref[:, (2 * head + m) * LANES:(2 * head + m + 1) * LANES] = km.astype(BF16)
            v_ref[head, 0, 0:V_DIM, :] = v[:, lanes].T.astype(BF16)
            v_ref[head, 0, V_DIM:ACC_ROWS, :] = _ones_rows(tm).astype(BF16)

    tail_ref[...] = carry_ref[...]


def _resident(shape):
    return pl.BlockSpec(shape, lambda i: (0,) * len(shape), pipeline_mode=pl.Buffered(1))


def _in_proj(h, gain, w_in, conv_w, hist, *, seq_rows, attn_layout):
    rows, d = h.shape
    width = w_in.shape[1] // N_SEG
    tm = min(ROW_TILE, rows)
    n_tiles = rows // tm
    assert rows % tm == 0 and seq_rows % tm == 0 and width % COL_CHUNK == 0
    assert width == N_HEADS * V_DIM and COL_CHUNK % V_DIM == 0
    row_spec = lambda n: pl.BlockSpec((tm, n), lambda i: (i, 0))
    act = jax.ShapeDtypeStruct((rows, width), BF16)
    if attn_layout:
        qkv_specs = [pl.BlockSpec((N_HEADS, 2, 1, LANES, tm), lambda i: (0, 0, i, 0, 0)),
                     row_spec(2 * width),
                     pl.BlockSpec((N_HEADS, 1, ACC_ROWS, tm), lambda i: (0, i, 0, 0))]
        qkv_shapes = [jax.ShapeDtypeStruct((N_HEADS, 2, n_tiles, LANES, tm), BF16),
                      jax.ShapeDtypeStruct((rows, 2 * width), BF16),
                      jax.ShapeDtypeStruct((N_HEADS, n_tiles, ACC_ROWS, tm), BF16)]
    else:
        qkv_specs, qkv_shapes = [row_spec(width)] * 3, [act] * 3
    return pl.pallas_call(
        functools.partial(_in_proj_body, tiles_per_seq=seq_rows // tm, width=width,
                          q_scale=HEAD_DIM ** -0.5 * (LOG2E if attn_layout else 1.0),
                          attn_layout=attn_layout),
        grid=(n_tiles,),
        in_specs=[row_spec(d), _resident((1, d)), _resident(w_in.shape), _resident(conv_w.shape),
                  _resident(hist.shape)],
        out_specs=[row_spec(width)] + qkv_specs + [row_spec(width)] * 2
                  + [pl.BlockSpec((SUBLANES, width), lambda i: (0, 0))],
        out_shape=[act] + qkv_shapes + [act] * 2 + [jax.ShapeDtypeStruct((SUBLANES, width), F32)],
        scratch_shapes=[pltpu.VMEM((tm, d), BF16),
                        pltpu.VMEM((CONV_BUFFERS, tm + SUBLANES, COL_CHUNK), F32),
                        pltpu.VMEM((SUBLANES, width), F32)],
        compiler_params=pltpu.CompilerParams(dimension_semantics=("arbitrary",),
                                             vmem_limit_bytes=VMEM_LIMIT),
        name="in_proj",
    )(h, gain, w_in, conv_w, hist)


def _lam(lam_ref, lam_init):
    l = lam_ref[...]
    return (jnp.exp(jnp.sum(l[0:1] * l[1:2], axis=-1, keepdims=True))
            - jnp.exp(jnp.sum(l[2:3] * l[3:4], axis=-1, keepdims=True)) + lam_init)


def _subln(o, g, lam_init):
    return _rms(o, g) * (1.0 - lam_init)


def _attn_body(qt_ref, ka_ref, vt_ref, mk_ref, mv_ref, lam_ref, g_ref, o_ref, mask_ref, *scratch,
               lam_init):
    n_heads, _, n_q, _, tq = qt_ref.shape
    n_k, _, tk = vt_ref.shape[1:]
    ratio = tq // tk
    n_items = ratio * n_q * (n_q + 1) // 2
    streams = [(hd, m) for hd in range(n_heads) for m in range(2)]
    m0_refs, acc0_refs, s_refs, tmax_refs, acc_refs = [], [], [], [], []
    for j in range(len(streams)):
        m0, acc0, s_a, s_b, tm_a, tm_b, acc = scratch[7 * j:7 * j + 7]
        m0_refs.append(m0)
        acc0_refs.append(acc0)
        s_refs.append((s_a, s_b))
        tmax_refs.append((tm_a, tm_b))
        acc_refs.append(acc)
    head_lanes = lambda hd: slice(hd * V_DIM, (hd + 1) * V_DIM)

    @pl.when((pl.program_id(0) == 0) & (pl.program_id(1) == 0))
    def _build_masks():
        kk = lax.broadcasted_iota(jnp.int32, (tk, tq), 0)
        qq = lax.broadcasted_iota(jnp.int32, (tk, tq), 1)
        for d in range(ratio):
            mask_ref[d] = jnp.where(kk + d * tk <= qq, 0.0, NEG)

    def stage_a(j, qi, t, slot, masked):
        hd, m = streams[j]
        r0 = pl.multiple_of(t * tk, tk)
        keys = ka_ref[pl.ds(r0, tk), (2 * hd + m) * LANES:(2 * hd + m + 1) * LANES]
        s = _dot(keys, qt_ref[hd, m, qi])
        if masked:
            s = s + mask_ref[jnp.maximum(t - qi * ratio, 0)]
        s_refs[j][slot][...] = s
        tmax_refs[j][slot][...] = jnp.max(s, axis=0, keepdims=True)

    feat, lane = _key_features(0, N_META)
    mk_aug, mvt = [], []
    for hd in range(n_heads):
        mk_aug.extend(a.astype(BF16) for a in _split_maps(mk_ref[:, head_lanes(hd)].astype(F32), feat, lane))
        mv_pad = jnp.concatenate([mv_ref[:, head_lanes(hd)].astype(F32),
                                  jnp.zeros((LANES - N_META, LANES), F32)], axis=0)
        mvt.append(jnp.concatenate([mv_pad.T[:, 0:N_META], _ones_rows(N_META)], axis=0).astype(BF16))

    pairs = [(j, i) for j in range(len(streams)) for i in range(n_q)]
    s_meta = [_dot(mk_aug[j], qt_ref[streams[j][0], streams[j][1], i]) for j, i in pairs]
    m0 = [jnp.max(s, axis=0, keepdims=True) for s in s_meta]
    p_meta = [jnp.exp2(s - mx).astype(BF16) for s, mx in zip(s_meta, m0)]
    for (j, i), mx, p in zip(pairs, m0, p_meta):
        m0_refs[j][i] = mx
        acc0_refs[j][i] = _dot(mvt[streams[j][0]], p)

    for j in range(len(streams)):
        for par in range(min(2, n_q)):
            acc_refs[j][par] = acc0_refs[j][par]

    def stage_b(j, qi, t, slot, m_prev):
        par = qi & 1
        m_old = jnp.where(t == 0, m0_refs[j][qi], m_prev)
        m_new = jnp.maximum(m_old, tmax_refs[j][slot][...])
        p = jnp.exp2(s_refs[j][slot][...] - m_new).astype(BF16)
        acc_refs[j][par] = (jnp.exp2(m_old - m_new) * acc_refs[j][par]
                            + _dot(vt_ref[streams[j][0], t], p))
        return m_new

    def stages(qa, ta, slot_a, masked, qb, tb, slot_b, ms):
        out = []
        for j in range(len(streams)):
            stage_a(j, qa, ta, slot_a, masked)
            out.append(stage_b(j, qb, tb, slot_b, ms[j]))
        return tuple(out)

    def finish_query_tile(qi):
        par = qi & 1
        r0 = pl.multiple_of(qi * tq, tq)
        nxt = jnp.minimum(qi + 2, n_q - 1)
        for hd in range(n_heads):
            a1, a2 = acc_refs[2 * hd], acc_refs[2 * hd + 1]
            w1 = 1.0 / a1[par, V_DIM:V_DIM + 1, :]
            w2 = _lam(lam_ref, lam_init) / a2[par, V_DIM:V_DIM + 1, :]
            o_t = a1[par, 0:V_DIM, :] * w1 - a2[par, 0:V_DIM, :] * w2
            scale = lax.rsqrt(jnp.mean(o_t * o_t, axis=0, keepdims=True) + RMS_EPS) * (1.0 - lam_init)
            o_ref[pl.ds(r0, tq), head_lanes(hd)] = ((o_t * scale).T * g_ref[...]).astype(BF16)
            for j in (2 * hd, 2 * hd + 1):
                acc_refs[j][par] = acc0_refs[j][nxt]

    def advance(qi, t):
        end = t + 1 == (qi + 1) * ratio
        more = qi + 1 < n_q
        return (jnp.where(end & more, qi + 1, qi),
                jnp.where(end, jnp.where(more, 0, t), t + 1))

    def iteration(slot_b, carry):
        qa, ta, qb, tb, ms = carry

        def run(masked, finish_previous):
            def f():
                if finish_previous:
                    finish_query_tile(qb - 1)
                return stages(qa, ta, 1 - slot_b, masked, qb, tb, slot_b, ms)
            return f

        masked = ta >= qa * ratio
        ms = lax.cond((tb == 0) & (qb > 0),
                      lambda: lax.cond(masked, run(True, True), run(False, True)),
                      lambda: lax.cond(masked, run(True, False), run(False, False)))
        qn, tn = advance(qa, ta)
        return qn, tn, qa, ta, ms

    zero = jnp.int32(0)
    for j in range(len(streams)):
        stage_a(j, zero, zero, 0, True)
    q1_, t1_ = advance(zero, zero)
    m_init = tuple(jnp.full((1, tq), NEG, F32) for _ in streams)
    carry = lax.fori_loop(0, n_items // 2, lambda i, c: iteration(1, iteration(0, c)),
                          (q1_, t1_, zero, zero, m_init))
    if n_items % 2:
        iteration(0, carry)
    finish_query_tile(jnp.int32(n_q - 1))


def _attention(qt, ka, vt, mk, mv, lam_vecs, g, *, batch, seq, lam_init):
    tq, tk = qt.shape[-1], vt.shape[-1]
    n_q, n_k = seq // tq, seq // tk
    hp = HEADS_PER_STEP
    assert seq % tq == 0 and tq % tk == 0 and tk % LANES == 0 and N_HEADS % hp == 0
    assert qt.shape[2] == batch * n_q and vt.shape[1] == batch * n_k
    full = lambda a: pl.BlockSpec(a.shape, lambda b, h: (0,) * a.ndim)
    meta_spec = pl.BlockSpec((N_META, hp * V_DIM), lambda b, h: (0, h))
    vmem = pltpu.VMEM
    per_stream = [vmem((n_q, 1, tq), F32), vmem((n_q, ACC_ROWS, tq), F32),
                  vmem((tk, tq), F32), vmem((tk, tq), F32), vmem((1, tq), F32), vmem((1, tq), F32),
                  vmem((2, ACC_ROWS, tq), F32)]
    return pl.pallas_call(
        functools.partial(_attn_body, lam_init=lam_init),
        grid=(batch, N_HEADS // hp),
        in_specs=[pl.BlockSpec((hp, 2, n_q, LANES, tq), lambda b, h: (h, 0, b, 0, 0)),
                  pl.BlockSpec((seq, hp * 2 * LANES), lambda b, h: (b, h)),
                  pl.BlockSpec((hp, n_k, ACC_ROWS, tk), lambda b, h: (h, b, 0, 0)),
                  meta_spec, meta_spec, full(lam_vecs), full(g)],
        out_specs=pl.BlockSpec((seq, hp * V_DIM), lambda b, h: (b, h)),
        out_shape=jax.ShapeDtypeStruct((batch * seq, N_HEADS * V_DIM), BF16),
        scratch_shapes=[vmem((tq // tk, tk, tq), F32)] + per_stream * (2 * hp),
        compiler_params=pltpu.CompilerParams(dimension_semantics=("arbitrary", "arbitrary"),
                                             vmem_limit_bytes=VMEM_LIMIT),
        name="diff_attention",
    )(qt, ka, vt, mk, mv, lam_vecs, g)


def _meta_attn_body(q_ref, k_ref, v_ref, slope_ref, lam_ref, g_ref, o_ref, *, lam_init):
    q = q_ref[...]
    k = k_ref[...]
    n = q.shape[0]
    lane = lax.broadcasted_iota(jnp.int32, q.shape, 1)
    zero = jnp.zeros_like(q)
    contract_last = (((1,), (1,)), ((), ()))
    s1 = lax.dot_general(jnp.where(lane < HEAD_DIM, q, zero), k, contract_last,
                         preferred_element_type=F32)
    s2 = lax.dot_general(jnp.where(lane >= HEAD_DIM, q, zero), k, contract_last,
                         preferred_element_type=F32)
    qpos = lax.broadcasted_iota(jnp.int32, (n, n), 0)
    kpos = lax.broadcasted_iota(jnp.int32, (n, n), 1)
    bias = -slope_ref[...][:, 0:1] * (qpos - kpos).astype(F32)

    def softmax(s):
        s = jnp.where(kpos <= qpos, s + bias, NEG)
        e = jnp.exp(s - jnp.max(s, axis=-1, keepdims=True))
        return e / jnp.sum(e, axis=-1, keepdims=True)

    w = softmax(s1) - _lam(lam_ref, lam_init) * softmax(s2)
    o_ref[...] = _subln(_dot(w.astype(BF16), v_ref[...]), g_ref[...], lam_init).astype(BF16)


def _meta_attention(q, k, v, slopes, lam_vecs, g, *, lam_init):
    full = lambda a: pl.BlockSpec(a.shape, lambda h: (0,) * a.ndim)
    head = pl.BlockSpec((N_META, V_DIM), lambda h: (0, h))
    return pl.pallas_call(
        functools.partial(_meta_attn_body, lam_init=lam_init),
        grid=(N_HEADS,),
        in_specs=[head, head, head, pl.BlockSpec((None, 1, LANES), lambda h: (h, 0, 0)),
                  full(lam_vecs), full(g)],
        out_specs=head,
        out_shape=jax.ShapeDtypeStruct(q.shape, BF16),
        name="meta_attention",
    )(q, k, v, slopes, lam_vecs, g)


def _merge_body(ypre_ref, o_ref, sga_ref, sgb_ref, h_ref, wc_ref, wa_ref, wm_ref, g_ref, out_ref,
                gated_ref, acc_ref):
    n_chunks = wm_ref.shape[0] // COL_CHUNK
    chunk = lambda j: slice(j * COL_CHUNK, (j + 1) * COL_CHUNK)

    def branches(j):
        ya = _dot(ypre_ref[...], wc_ref[:, chunk(j)])
        yb = _dot(o_ref[...], wa_ref[:, chunk(j)])
        gated = sga_ref[:, chunk(j)].astype(F32) * ya + sgb_ref[:, chunk(j)].astype(F32) * yb
        gated_ref[j % 2] = gated.astype(BF16)

    def mix(j):
        y = _dot(gated_ref[j % 2], wm_ref[chunk(j), :])
        if j == 0:
            acc_ref[...] = y
        else:
            acc_ref[...] += y

    branches(0)
    for j in range(n_chunks):
        if j + 1 < n_chunks:
            branches(j + 1)
        mix(j)
    out_ref[...] = h_ref[...] + _rms(acc_ref[...], g_ref[...])


def _merge(ypre, o, sga, sgb, h, wc, wa, wm, g):
    rows, d = h.shape
    tm = min(2 * ROW_TILE, rows)
    assert rows % tm == 0
    row_spec = lambda n: pl.BlockSpec((tm, n), lambda i: (i, 0))
    return pl.pallas_call(
        _merge_body,
        grid=(rows // tm,),
        in_specs=[row_spec(ypre.shape[1]), row_spec(o.shape[1]), row_spec(sga.shape[1]),
                  row_spec(sgb.shape[1]), row_spec(d),
                  _resident(wc.shape), _resident(wa.shape), _resident(wm.shape), _resident(g.shape)],
        out_specs=row_spec(d),
        out_shape=jax.ShapeDtypeStruct((rows, d), F32),
        scratch_shapes=[pltpu.VMEM((2, tm, COL_CHUNK), BF16), pltpu.VMEM((tm, d), F32)],
        compiler_params=pltpu.CompilerParams(dimension_semantics=("arbitrary",),
                                             vmem_limit_bytes=VMEM_LIMIT),
        name="merge",
    )(ypre, o, sga, sgb, h, wc, wa, wm, g)


def _ffn_body(h_ref, gpre_ref, wup_ref, cw_ref, cb_ref, hist_ref, wdn_ref, gpost_ref,
              out_ref, tail_ref, f_ref, buf_ref, carry_ref, acc_ref, a_ref, *, tiles_per_seq, d_ff):
    @pl.when(pl.program_id(0) % tiles_per_seq == 0)
    def _():
        carry_ref[...] = hist_ref[...]

    tm = h_ref.shape[0]
    n_chunks = d_ff // COL_CHUNK
    half_cols = lambda c, half: slice(half * d_ff + c * COL_CHUNK, half * d_ff + (c + 1) * COL_CHUNK)
    half_buf = lambda c, half: buf_ref.at[(2 * c + half) % CONV_BUFFERS]

    def norm_rows(rows):
        f_ref[rows, :] = _rms(h_ref[rows, :], gpre_ref[...]).astype(BF16)

    head_rows = tm // 2 if tm % 32 == 0 else tm
    norm_rows(slice(0, head_rows))

    def up(c):
        for half in range(2):
            cols = half_cols(c, half)
            if c == 0 and half == 0 and head_rows < tm:
                top = _dot(f_ref[0:head_rows, :], wup_ref[:, cols])
                norm_rows(slice(head_rows, tm))
                z = jnp.concatenate([top, _dot(f_ref[head_rows:tm, :], wup_ref[:, cols])], axis=0)
            else:
                z = _dot(f_ref[...], wup_ref[:, cols])
            _conv_stage(z, half_buf(c, half), carry_ref, cols)

    def gate(c):
        g, u = (_conv_apply(cw_ref[:, half_cols(c, half)], half_buf(c, half)) + cb_ref[:, half_cols(c, half)]
                for half in range(2))
        a_ref[c % (2 * DOWN_GROUP)] = (jax.nn.silu(g) * u).astype(BF16)

    def down(c0, n, rows=slice(0, tm)):
        a = jnp.concatenate([a_ref[(c0 + i) % (2 * DOWN_GROUP), rows, :] for i in range(n)], axis=1)
        y = _dot(a, wdn_ref[c0 * COL_CHUNK:(c0 + n) * COL_CHUNK, :])
        if c0 == 0:
            acc_ref[rows, :] = y
        else:
            acc_ref[rows, :] += y

    def last_down(c0, n):
        parts = [slice(0, head_rows), slice(head_rows, tm)] if head_rows < tm else [slice(0, tm)]
        for rows in parts:
            down(c0, n, rows)
        for rows in parts:
            out_ref[rows, :] = h_ref[rows, :] + _rms(acc_ref[rows, :], gpost_ref[...])

    assert CONV_BUFFERS >= 6
    up(0)
    up(1)
    for c in range(n_chunks + 1):
        if c + 2 < n_chunks:
            up(c + 2)
        if c >= DOWN_GROUP and c % DOWN_GROUP == 0:
            (last_down if c == n_chunks else down)(c - DOWN_GROUP, DOWN_GROUP)
        if c < n_chunks:
            gate(c)
    if n_chunks % DOWN_GROUP:
        last_down(n_chunks - n_chunks % DOWN_GROUP, n_chunks % DOWN_GROUP)

    tail_ref[...] = carry_ref[...]


def _ffn(h, gpre, wup, conv_w, conv_b, hist, wdn, gpost, *, seq_rows):
    rows, d = h.shape
    d_ff = wdn.shape[0]
    tm = min(ROW_TILE, rows)
    assert rows % tm == 0 and seq_rows % tm == 0 and d_ff % COL_CHUNK == 0
    row_spec = pl.BlockSpec((tm, d), lambda i: (i, 0))
    return pl.pallas_call(
        functools.partial(_ffn_body, tiles_per_seq=seq_rows // tm, d_ff=d_ff),
        grid=(rows // tm,),
        in_specs=[row_spec, _resident(gpre.shape), _resident(wup.shape), _resident(conv_w.shape),
                  _resident(conv_b.shape), _resident(hist.shape), _resident(wdn.shape),
                  _resident(gpost.shape)],
        out_specs=[row_spec, pl.BlockSpec((SUBLANES, 2 * d_ff), lambda i: (0, 0))],
        out_shape=[jax.ShapeDtypeStruct((rows, d), F32),
                   jax.ShapeDtypeStruct((SUBLANES, 2 * d_ff), F32)],
        scratch_shapes=[pltpu.VMEM((tm, d), BF16),
                        pltpu.VMEM((CONV_BUFFERS, tm + SUBLANES, COL_CHUNK), F32),
                        pltpu.VMEM((SUBLANES, 2 * d_ff), F32),
                        pltpu.VMEM((tm, d), F32),
                        pltpu.VMEM((2 * DOWN_GROUP, tm, COL_CHUNK), BF16)],
        compiler_params=pltpu.CompilerParams(dimension_semantics=("arbitrary",),
                                             vmem_limit_bytes=VMEM_LIMIT),
        name="ffn",
    )(h, gpre, wup, conv_w, conv_b, hist, wdn, gpost)


def kernel(x, meta_tokens, w_in, conv_w, w_conv_out, lambda_q1, lambda_k1, lambda_q2, lambda_k2,
           subln_g, w_attn_out, w_mix_out, norm_mix_pre, norm_mix_post, w_ffn_up, ffn_conv_w,
           ffn_conv_b, w_ffn_down, norm_ffn_pre, norm_ffn_post):
    batch, seq, d = x.shape
    depth = w_in.shape[0]
    assert meta_tokens.shape[0] == N_META and seq + N_META <= POS_SPLIT * 256 and seq % ROW_TILE == 0
    slopes = jnp.asarray(np.broadcast_to(_alibi_slopes()[:, None, None], (N_HEADS, 1, LANES)), F32)
    row = lambda a: a.reshape(1, -1)

    hx = x.reshape(batch * seq, d)
    hm = meta_tokens.astype(x.dtype)
    for l in range(depth):
        lam_init = 0.8 - 0.6 * math.exp(-0.3 * l)
        w_in_l = w_in[l].astype(BF16)
        lam_vecs = jnp.stack([lambda_q1[l], lambda_k1[l], lambda_q2[l], lambda_k2[l]]).astype(F32)
        g_sub = row(subln_g[l])
        zero_hist = jnp.zeros((SUBLANES, conv_w.shape[2]), F32)

        ypre_m, q_m, k_m, v_m, sga_m, sgb_m, u_tail = _in_proj(
            hm, row(norm_mix_pre[l]), w_in_l, conv_w[l], zero_hist, seq_rows=N_META, attn_layout=False)
        ypre, qt, ka, vt, sga, sgb, _ = _in_proj(
            hx, row(norm_mix_pre[l]), w_in_l, conv_w[l], u_tail, seq_rows=seq, attn_layout=True)

        o_m = _meta_attention(q_m, k_m, v_m, slopes, lam_vecs, g_sub, lam_init=lam_init)
        o = _attention(qt, ka, vt, k_m, v_m, lam_vecs, g_sub, batch=batch, seq=seq, lam_init=lam_init)

        wc, wa, wm = (w.astype(BF16) for w in (w_conv_out[l], w_attn_out[l], w_mix_out[l]))
        hm = _merge(ypre_m, o_m, sga_m, sgb_m, hm, wc, wa, wm, row(norm_mix_post[l]))
        hx = _merge(ypre, o, sga, sgb, hx, wc, wa, wm, row(norm_mix_post[l]))

        wup, wdn = w_ffn_up[l].astype(BF16), w_ffn_down[l].astype(BF16)
        zero_hist = jnp.zeros((SUBLANES, wup.shape[1]), F32)
        ffn = functools.partial(_ffn, gpre=row(norm_ffn_pre[l]), wup=wup, conv_w=ffn_conv_w[l],
                                conv_b=row(ffn_conv_b[l]), wdn=wdn, gpost=row(norm_ffn_post[l]))
        hm, z_tail = ffn(hm, hist=zero_hist, seq_rows=N_META)
        hx, _ = ffn(hx, hist=z_tail, seq_rows=seq)
    return hx.reshape(batch, seq, d)
```
